```python
import jax, jax.numpy as jnp
from jax import lax
import numpy as np

D_MODEL = 2048
BATCH = 8
SEQ = 2048
DEPTH = 1
DEC_BATCH = 32
DEC_SEQ = 4
PAST_LEN = 8192
PAGE_SIZE = 128

N_HEADS = 8
N_KV_HEADS = 2
HEAD_DIM = 128
IDX_HEADS = 16
IDX_DIM = 64
TOPK_MAX = 256
Q_BLOCK = 128
D_CONV = 1024
CONV_WIDTH = 31
N_EXPERTS = 64
N_GROUPS = 8
TOPK_GROUPS = 4
TOP_K = 8
MOE_HIDDEN = 512
SHARED_HIDDEN = 512
ROUTED_SCALE = 2.5
MOE_BLOCK = 128
EPS = 1e-6

D_Q = N_HEADS * HEAD_DIM
D_KV = N_KV_HEADS * HEAD_DIM
D_QI = IDX_HEADS * IDX_DIM
SPLITS = (D_Q, D_KV, D_KV, D_QI, IDX_DIM, IDX_HEADS, 2 * D_CONV, D_MODEL, D_MODEL)
D_IN = sum(SPLITS)
SPLIT_OFFSETS = tuple(int(o) for o in np.cumsum(SPLITS)[:-1])

kernel_name = 'hybrid_dsa_conformer_moe_step'


def _rmsnorm(x, g):
    xf = x.astype(jnp.float32)
    y = xf * lax.rsqrt(jnp.mean(xf * xf, axis=-1, keepdims=True) + EPS)
    return (y * g.astype(jnp.float32)).astype(x.dtype)


def _layernorm(x, g, b):
    xf = x.astype(jnp.float32)
    mu = jnp.mean(xf, axis=-1, keepdims=True)
    var = jnp.mean(jnp.square(xf - mu), axis=-1, keepdims=True)
    y = (xf - mu) * lax.rsqrt(var + EPS) * g.astype(jnp.float32) + b.astype(jnp.float32)
    return y.astype(x.dtype)


def _modulation(c, w_ada, b_ada):
    m = (c @ w_ada + b_ada).reshape(c.shape[0], 1, 6, D_MODEL)
    return [m[:, :, i] for i in range(6)]


def _mix_inputs(x, shift, scale, g_pre, w_in):
    B, T, _ = x.shape
    h = _rmsnorm(x, g_pre) * (1 + scale) + shift
    q, k, v, qi, ki, wi, conv_in, ga, gb = jnp.split(h @ w_in, SPLIT_OFFSETS, axis=-1)
    q = q.reshape(B, T, N_HEADS, HEAD_DIM)
    k = k.reshape(B, T, N_KV_HEADS, HEAD_DIM)
    v = v.reshape(B, T, N_KV_HEADS, HEAD_DIM)
    qi = qi.reshape(B, T, IDX_HEADS, IDX_DIM)
    wi = wi * IDX_HEADS ** -0.5
    a, b = jnp.split(conv_in, 2, axis=-1)
    u = a * jax.nn.sigmoid(b)
    return q, k, v, qi, ki, wi, u, ga, gb


def _index_scores(qi, wi, ki, q_pos, k_pos):
    dots = jnp.einsum('bthd,bsd->bths', qi, ki) * IDX_DIM ** -0.5
    s = jnp.einsum('bth,bths->bts', wi, jax.nn.relu(dots)).astype(jnp.float32)
    return jnp.where(k_pos[None, None, :] <= q_pos[None, :, None], s, -jnp.inf)


def _select(scores, q_pos, topk):
    _, idx = lax.top_k(scores, topk)
    valid = idx <= q_pos[None, :, None]
    return idx, valid


def _sparse_attend(q, k_sel, v_sel, valid):
    B, T = q.shape[:2]
    qg = q.reshape(B, T, N_KV_HEADS, N_HEADS // N_KV_HEADS, HEAD_DIM)
    s = jnp.einsum('btngd,btsnd->btngs', qg, k_sel).astype(jnp.float32) * HEAD_DIM ** -0.5
    s = jnp.where(valid[:, :, None, None, :], s, -jnp.inf)
    p = jax.nn.softmax(s, axis=-1).astype(v_sel.dtype)
    o = jnp.einsum('btngs,btsnd->btngd', p, v_sel)
    return o.reshape(B, T, D_Q)


def _attn_prompt(q, k, v, qi, wi, ki):
    B, S = q.shape[:2]
    topk = min(TOPK_MAX, S // 4)
    n_blk = S // Q_BLOCK
    k_pos = jnp.arange(S)
    gather = jax.vmap(lambda a, i: a[i])

    def to_blocks(a):
        return jnp.moveaxis(a.reshape(B, n_blk, Q_BLOCK, *a.shape[2:]), 1, 0)

    def one_block(args):
        qb, qib, wib, pos = args
        sc = _index_scores(qib, wib, ki, pos, k_pos)
        idx, valid = _select(sc, pos, topk)
        return _sparse_attend(qb, gather(k, idx), gather(v, idx), valid)

    pos_blocks = jnp.arange(S).reshape(n_blk, Q_BLOCK)
    out = lax.map(one_block, (to_blocks(q), to_blocks(qi), to_blocks(wi), pos_blocks))
    return jnp.moveaxis(out, 0, 1).reshape(B, S, D_Q)


def _attn_sample(q, k_new, v_new, qi, wi, ki_new, cache_k, cache_v, cache_kidx, page_table):
    DB, T = q.shape[:2]
    P = page_table.shape[1] * PAGE_SIZE
    L = P + T
    topk = min(TOPK_MAX, L // 4)
    ki_past = cache_kidx[page_table].reshape(DB, P, IDX_DIM)
    ki_all = jnp.concatenate([ki_past, ki_new.astype(ki_past.dtype)], axis=1)
    q_pos = P + jnp.arange(T)
    sc = _index_scores(qi, wi, ki_all, q_pos, jnp.arange(L))
    idx, valid = _select(sc, q_pos, topk)
    in_past = idx < P
    pidx = jnp.minimum(idx, P - 1)
    phys = jnp.take_along_axis(page_table, (pidx // PAGE_SIZE).reshape(DB, -1), axis=1).reshape(idx.shape)
    off = pidx % PAGE_SIZE
    nidx = jnp.clip(idx - P, 0, T - 1)
    gather_new = jax.vmap(lambda a, i: a[i])

    def pick(cache, new):
        past = cache[phys, off]
        cur = gather_new(new, nidx).astype(past.dtype)
        return jnp.where(in_past[..., None, None], past, cur)

    return _sparse_attend(q, pick(cache_k, k_new), pick(cache_v, v_new), valid)


def _conv_branch(u_ext, w_dw, b_dw, ln_g, ln_b, w_pw2):
    y = lax.conv_general_dilated(u_ext, w_dw[:, None, :], window_strides=(1,), padding='VALID',
                                 dimension_numbers=('NWC', 'WIO', 'NWC'),
                                 feature_group_count=D_CONV) + b_dw
    y = _layernorm(y, ln_g, ln_b)
    return jax.nn.silu(y) @ w_pw2


def _mix_output(x, attn_o, conv_o, ga, gb, gate, w_oa, w_o, g_post):
    merged = jax.nn.sigmoid(ga) * (attn_o @ w_oa) + jax.nn.sigmoid(gb) * conv_o
    return x + gate * _rmsnorm(merged @ w_o, g_post)


def _swiglu(x, w_gate, w_up, w_down):
    return (jax.nn.silu(x @ w_gate) * (x @ w_up)) @ w_down


def _route(h, w_router, b_router):
    N = h.shape[0]
    s = jax.nn.sigmoid((h @ w_router).astype(jnp.float32))
    sel = s + b_router.astype(jnp.float32)
    grp = sel.reshape(N, N_GROUPS, N_EXPERTS // N_GROUPS)
    gscore = lax.top_k(grp, 2)[0].sum(-1)
    _, gidx = lax.top_k(gscore, TOPK_GROUPS)
    gmask = jax.nn.one_hot(gidx, N_GROUPS, dtype=jnp.float32).sum(1) > 0
    sel = jnp.where(jnp.repeat(gmask, N_EXPERTS // N_GROUPS, axis=1), sel, -jnp.inf)
    _, eidx = lax.top_k(sel, TOP_K)
    w = jnp.take_along_axis(s, eidx, axis=1)
    w = w / jnp.sum(w, axis=-1, keepdims=True) * ROUTED_SCALE
    return eidx, w


def _moe_routed(h, eidx, ew, w_gate, w_up, w_down):
    N, D = h.shape
    M = N * TOP_K
    flat_e = eidx.reshape(M)
    order = jnp.argsort(flat_e)
    e_sorted = flat_e[order]
    tok = order // TOP_K
    counts = jnp.bincount(flat_e, length=N_EXPERTS)
    padded = (counts + MOE_BLOCK - 1) // MOE_BLOCK * MOE_BLOCK
    pad_end = jnp.cumsum(padded)
    start = jnp.cumsum(counts) - counts
    dest = (pad_end - padded)[e_sorted] + jnp.arange(M) - start[e_sorted]
    n_blocks = -(-M // MOE_BLOCK) + N_EXPERTS
    rows = jnp.zeros((n_blocks * MOE_BLOCK,), jnp.int32).at[dest].set(tok)
    blk_e = jnp.minimum(jnp.searchsorted(pad_end, jnp.arange(n_blocks) * MOE_BLOCK, side='right'),
                        N_EXPERTS - 1)
    xb = h[rows].reshape(n_blocks, MOE_BLOCK, D)

    def expert_block(args):
        xe, e = args
        return _swiglu(xe, w_gate[e], w_up[e], w_down[e])

    yb = lax.map(expert_block, (xb, blk_e)).reshape(n_blocks * MOE_BLOCK, D)
    contrib = yb[dest] * ew.reshape(M)[order][:, None].astype(yb.dtype)
    return jax.ops.segment_sum(contrib, tok, num_segments=N)


def _ffn_sublayer(x, shift, scale, gate, g_pre, g_post, w_router, b_router,
                  w_exp_gate, w_exp_up, w_exp_down, w_sh_gate, w_sh_up, w_sh_down):
    B, T, D = x.shape
    h = (_rmsnorm(x, g_pre) * (1 + scale) + shift).reshape(B * T, D)
    eidx, ew = _route(h, w_router, b_router)
    y = _swiglu(h, w_sh_gate, w_sh_up, w_sh_down) + _moe_routed(h, eidx, ew, w_exp_gate, w_exp_up, w_exp_down)
    return x + gate * _rmsnorm(y.reshape(B, T, D), g_post)


def setup_inputs(seed: int = 0) -> dict:
    key = jax.random.key(seed)
    ks = iter(jax.random.split(key, 48))

    def nrm(shape, scale=1.0):
        return jax.random.normal(next(ks), shape, jnp.float32) * scale

    n_pages = PAST_LEN // PAGE_SIZE
    used = DEC_BATCH * n_pages
    n_pool = used + max(1, used // 4)
    L = DEPTH
    page_table = jax.random.permutation(next(ks), n_pool)[:used].reshape(DEC_BATCH, n_pages).astype(jnp.int32)
    return {
        'x_prompt': nrm((BATCH, SEQ, D_MODEL)),
        'x_sample': nrm((DEC_BATCH, DEC_SEQ, D_MODEL)),
        'cache_k': nrm((L, n_pool, PAGE_SIZE, N_KV_HEADS, HEAD_DIM)),
        'cache_v': nrm((L, n_pool, PAGE_SIZE, N_KV_HEADS, HEAD_DIM)),
        'cache_kidx': nrm((L, n_pool, PAGE_SIZE, IDX_DIM)),
        'state_conv': nrm((L, DEC_BATCH, CONV_WIDTH - 1, D_CONV), 0.5),
        'page_table': page_table,
        'c_prompt': nrm((BATCH, D_MODEL)),
        'c_sample': nrm((DEC_BATCH, D_MODEL)),
        'w_ada': nrm((L, D_MODEL, 6 * D_MODEL), 0.2 * D_MODEL ** -0.5),
        'b_ada': nrm((L, 6 * D_MODEL), 0.01),
        'g_mix_pre': 1.0 + nrm((L, D_MODEL), 0.01),
        'g_mix_post': 1.0 + nrm((L, D_MODEL), 0.01),
        'w_in': nrm((L, D_MODEL, D_IN), D_MODEL ** -0.5),
        'w_oa': nrm((L, D_Q, D_MODEL), D_Q ** -0.5),
        'w_dw': nrm((L, CONV_WIDTH, D_CONV), CONV_WIDTH ** -0.5),
        'b_dw': nrm((L, D_CONV), 0.01),
        'ln_conv_g': 1.0 + nrm((L, D_CONV), 0.01),
        'ln_conv_b': nrm((L, D_CONV), 0.01),
        'w_pw2': nrm((L, D_CONV, D_MODEL), D_CONV ** -0.5),
        'w_o': nrm((L, D_MODEL, D_MODEL), D_MODEL ** -0.5),
        'g_ffn_pre': 1.0 + nrm((L, D_MODEL), 0.01),
        'g_ffn_post': 1.0 + nrm((L, D_MODEL), 0.01),
        'w_router': nrm((L, D_MODEL, N_EXPERTS), D_MODEL ** -0.5),
        'b_router': nrm((L, N_EXPERTS), 0.01),
        'w_exp_gate': nrm((L, N_EXPERTS, D_MODEL, MOE_HIDDEN), D_MODEL ** -0.5),
        'w_exp_up': nrm((L, N_EXPERTS, D_MODEL, MOE_HIDDEN), D_MODEL ** -0.5),
        'w_exp_down': nrm((L, N_EXPERTS, MOE_HIDDEN, D_MODEL), MOE_HIDDEN ** -0.5),
        'w_sh_gate': nrm((L, D_MODEL, SHARED_HIDDEN), D_MODEL ** -0.5),
        'w_sh_up': nrm((L, D_MODEL, SHARED_HIDDEN), D_MODEL ** -0.5),
        'w_sh_down': nrm((L, SHARED_HIDDEN, D_MODEL), SHARED_HIDDEN ** -0.5),
    }


def reference(x_prompt, x_sample, cache_k, cache_v, cache_kidx, state_conv, page_table, c_prompt, c_sample,
              w_ada, b_ada, g_mix_pre, g_mix_post, w_in, w_oa, w_dw, b_dw, ln_conv_g, ln_conv_b, w_pw2, w_o,
              g_ffn_pre, g_ffn_post, w_router, b_router, w_exp_gate, w_exp_up, w_exp_down,
              w_sh_gate, w_sh_up, w_sh_down):
    xp, xs = x_prompt, x_sample
    kp_l, vp_l, ip_l, cp_l = [], [], [], []
    ks_l, vs_l, is_l, cs_l = [], [], [], []
    for l in range(DEPTH):
        mp = _modulation(c_prompt, w_ada[l], b_ada[l])
        ms = _modulation(c_sample, w_ada[l], b_ada[l])

        q, k, v, qi, ki, wi, u, ga, gb = _mix_inputs(xp, mp[0], mp[1], g_mix_pre[l], w_in[l])
        ao = _attn_prompt(q, k, v, qi, wi, ki)
        u_ext = jnp.pad(u, ((0, 0), (CONV_WIDTH - 1, 0), (0, 0)))
        co = _conv_branch(u_ext, w_dw[l], b_dw[l], ln_conv_g[l], ln_conv_b[l], w_pw2[l])
        xp = _mix_output(xp, ao, co, ga, gb, mp[2], w_oa[l], w_o[l], g_mix_post[l])
        xp = _ffn_sublayer(xp, mp[3], mp[4], mp[5], g_ffn_pre[l], g_ffn_post[l], w_router[l], b_router[l],
                           w_exp_gate[l], w_exp_up[l], w_exp_down[l], w_sh_gate[l], w_sh_up[l], w_sh_down[l])
        kp_l.append(k)
        vp_l.append(v)
        ip_l.append(ki)
        cp_l.append(u[:, -(CONV_WIDTH - 1):])

        q, k, v, qi, ki, wi, u, ga, gb = _mix_inputs(xs, ms[0], ms[1], g_mix_pre[l], w_in[l])
        ao = _attn_sample(q, k, v, qi, wi, ki, cache_k[l], cache_v[l], cache_kidx[l], page_table)
        u_ext = jnp.concatenate([state_conv[l].astype(u.dtype), u], axis=1)
        co = _conv_branch(u_ext, w_dw[l], b_dw[l], ln_conv_g[l], ln_conv_b[l], w_pw2[l])
        xs = _mix_output(xs, ao, co, ga, gb, ms[2], w_oa[l], w_o[l], g_mix_post[l])
        xs = _ffn_sublayer(xs, ms[3], ms[4], ms[5], g_ffn_pre[l], g_ffn_post[l], w_router[l], b_router[l],
                           w_exp_gate[l], w_exp_up[l], w_exp_down[l], w_sh_gate[l], w_sh_up[l], w_sh_down[l])
        ks_l.append(k)
        vs_l.append(v)
        is_l.append(ki)
        cs_l.append(u_ext[:, -(CONV_WIDTH - 1):])

    k_prompt = jnp.stack(kp_l)
    v_prompt = jnp.stack(vp_l)
    kidx_prompt = jnp.stack(ip_l)
    conv_prompt = jnp.stack(cp_l)
    k_sample = jnp.stack(ks_l)
    v_sample = jnp.stack(vs_l)
    kidx_sample = jnp.stack(is_l)
    conv_sample = jnp.stack(cs_l)
    return (xp, xs, k_prompt, v_prompt, kidx_prompt, conv_prompt, k_sample, v_sample, kidx_sample, conv_sample)
```

```python
import functools

import jax
import jax.numpy as jnp
from jax import lax
from jax.experimental import pallas as pl
from jax.experimental.pallas import tpu as pltpu

F32 = jnp.float32
BF16 = jnp.bfloat16
I32 = jnp.int32

N_HEADS = 8
N_KV_HEADS = 2
HEAD_DIM = 128
IDX_HEADS = 16
IDX_DIM = 64
TOPK_MAX = 256
CONV_WIDTH = 31
N_EXPERTS = 64
N_GROUPS = 8
TOPK_GROUPS = 4
TOP_K = 8
ROUTED_SCALE = 2.5
PAGE_SIZE = 128
EPS = 1e-6

D_Q = N_HEADS * HEAD_DIM
D_KV = N_KV_HEADS * HEAD_DIM
D_QI = IDX_HEADS * IDX_DIM
GROUP = N_HEADS // N_KV_HEADS
KIWI = 128
HALO = 32
EXPERT_BLOCK = 256
IDX_LANES = 128
INT_MIN = -(2 ** 31)
NEG_INF = float("-inf")
NEG_INF_KEY = -2139095041
VMEM_LIMIT = 56 * 1024 * 1024

NT_DIMS = (((1,), (1,)), ((), ()))


def _cparams(*sem):
    return pltpu.CompilerParams(dimension_semantics=sem, vmem_limit_bytes=VMEM_LIMIT)


def _round_up(x, m):
    return (x + m - 1) // m * m


def _rms(x, g):
    return x * lax.rsqrt(jnp.mean(x * x, axis=-1, keepdims=True) + EPS) * g


def _sigmoid(x):
    return 1.0 / (1.0 + jnp.exp(-x))


def _silu(x):
    return x * _sigmoid(x)


def _sortable(x):
    bits = pltpu.bitcast(x + 0.0, I32)
    return bits ^ ((bits >> 31) & 0x7FFFFFFF)


def _kth_largest_key(key_ref, k):
    rows = key_ref.shape[0]

    def body(it, prefix):
        cand = prefix | lax.shift_left(jnp.int32(1), 31 - it)
        cnt = jnp.sum((key_ref[...] >= (cand ^ INT_MIN)).astype(I32), axis=1, keepdims=True)
        return jnp.where(cnt >= k, cand, prefix)

    prefix = lax.fori_loop(0, 32, body, jnp.zeros((rows, 1), I32))
    return prefix ^ INT_MIN


def _topk_bias(key_ref, bias_ref, valid_fn, k):
    rows, cols = key_ref.shape
    thr = _kth_largest_key(key_ref, k)
    key = key_ref[...]
    ge = key >= thr
    bias_ref[...] = jnp.where(valid_fn(0, cols), jnp.where(ge, 0.0, NEG_INF), NEG_INF)
    n_ge = jnp.sum(ge.astype(I32), axis=1, keepdims=True)
    tied = jnp.where(n_ge > k, jnp.where(thr > NEG_INF_KEY, 1, 0), 0)

    @pl.when(jnp.max(tied) > 0)
    def _():
        need = (k - jnp.sum((key > thr).astype(I32), axis=1, keepdims=True)).astype(F32)
        r = lax.broadcasted_iota(I32, (128, 128), 0)
        c = lax.broadcasted_iota(I32, (128, 128), 1)
        tri = jnp.where(r < c, 1.0, 0.0).astype(BF16)
        carry = jnp.zeros((rows, 1), F32)
        for j in range(cols // 128):
            sl = slice(j * 128, (j + 1) * 128)
            key_c = key_ref[:, sl]
            eq_c = jnp.where(key_c == thr, 1.0, 0.0)
            before = carry + jnp.dot(eq_c.astype(BF16), tri, preferred_element_type=F32)
            keep_tied = jnp.where(key_c > thr, 1.0, jnp.where(before < need, eq_c, 0.0))
            keep = jnp.where(tied > 0, keep_tied, jnp.where(key_c >= thr, 1.0, 0.0))
            bias_ref[:, sl] = jnp.where(valid_fn(j * 128, 128), jnp.where(keep > 0.5, 0.0, NEG_INF), NEG_INF)
            carry = carry + jnp.sum(eq_c, axis=1, keepdims=True)


def _ada_kernel(c_ref, w_ref, b_ref, o_ref):
    o_ref[...] = jnp.dot(c_ref[...].astype(BF16), w_ref[...].astype(BF16),
                         preferred_element_type=F32) + b_ref[...]


def _ada(c, w_ada, b_ada):
    n, d = c.shape
    cols = w_ada.shape[1]
    tn = _pick(cols, (1024, 512, 256, 128))
    return pl.pallas_call(
        _ada_kernel,
        out_shape=jax.ShapeDtypeStruct((n, cols), F32),
        grid=(cols // tn,),
        in_specs=[pl.BlockSpec((n, d), lambda j: (0, 0)),
                  pl.BlockSpec((d, tn), lambda j: (0, j)),
                  pl.BlockSpec((1, tn), lambda j: (0, j))],
        out_specs=pl.BlockSpec((n, tn), lambda j: (0, j)),
        compiler_params=_cparams("parallel"),
        name="ada",
    )(c, w_ada, b_ada.reshape(1, cols))


class _Mod:
    def __init__(self, arr, per_token, rows_per_seq=None):
        self.arr = arr
        self.per_token = per_token
        self.rows_per_seq = rows_per_seq

    def spec(self, which, tm, d, ngrid):
        if self.per_token:
            if ngrid == 1:
                return pl.BlockSpec((tm, d), lambda i: (i, which))
            return pl.BlockSpec((tm, d), lambda i, j: (i, which))
        per = self.rows_per_seq // tm
        if ngrid == 1:
            return pl.BlockSpec((None, None, 1, d), lambda i: (i // per, which, 0, 0))
        return pl.BlockSpec((None, None, 1, d), lambda i, j: (i // per, which, 0, 0))


def _proj_kernel(x_ref, shift_ref, scale_ref, g_ref, w_ref, o_ref, h_ref):
    @pl.when(pl.program_id(1) == 0)
    def _():
        y = _rms(x_ref[...], g_ref[...])
        h_ref[...] = (y * (1.0 + scale_ref[...]) + shift_ref[...]).astype(BF16)

    o_ref[...] = jnp.dot(h_ref[...], w_ref[...], preferred_element_type=F32)


def _proj(x, mod, g, w, tm, tn):
    n, d = x.shape
    cols = w.shape[1]
    return pl.pallas_call(
        _proj_kernel,
        out_shape=jax.ShapeDtypeStruct((n, cols), F32),
        grid=(n // tm, cols // tn),
        in_specs=[pl.BlockSpec((tm, d), lambda i, j: (i, 0)),
                  mod.spec(0, tm, d, 2),
                  mod.spec(1, tm, d, 2),
                  pl.BlockSpec((1, d), lambda i, j: (0, 0)),
                  pl.BlockSpec((d, tn), lambda i, j: (0, j))],
        out_specs=pl.BlockSpec((tm, tn), lambda i, j: (i, j)),
        scratch_shapes=[pltpu.VMEM((tm, d), BF16)],
        compiler_params=_cparams("parallel", "arbitrary"),
        name="proj",
    )(x, mod.arr, mod.arr, g, w)


class _Layout:
    def __init__(self, d, dc):
        self.q = 0
        self.qi = D_Q
        self.ga = D_Q + D_QI
        self.gb = self.ga + d
        self.a = self.gb + d
        self.b = self.a + dc
        self.k = self.b + dc
        self.v = self.k + D_KV
        self.kiwi = self.v + D_KV
        used = self.kiwi + KIWI
        self.tn = 1024 if used >= 4096 else 512
        self.total = _round_up(used, self.tn)
        assert self.ga % d == 0 and self.a % dc == 0 and self.k % D_KV == 0


def _permute_w_in(w_in, d, dc, lay):
    offs = [0]
    for s in (D_Q, D_KV, D_KV, D_QI, IDX_DIM, IDX_HEADS, 2 * dc, d, d):
        offs.append(offs[-1] + s)
    q, k, v, qi, ki, wi, conv, ga, gb = [w_in[:, offs[i]:offs[i + 1]] for i in range(9)]
    pad1 = jnp.zeros((d, KIWI - IDX_DIM - IDX_HEADS), w_in.dtype)
    pad2 = jnp.zeros((d, lay.total - lay.kiwi - KIWI), w_in.dtype)
    return jnp.concatenate([q, qi, ga, gb, conv, k, v, ki, wi, pad1, pad2], axis=1).astype(BF16)


def _index_scores(qi, w, kke, kko):
    t = qi.shape[0]
    pairs = IDX_HEADS // 2
    qs = jnp.concatenate([qi[:, p * 128:(p + 1) * 128] for p in range(pairs)], axis=0)
    de = lax.dot_general(qs, kke, NT_DIMS, preferred_element_type=F32)
    do = lax.dot_general(qs, kko, NT_DIMS, preferred_element_type=F32)
    acc = None
    for p in range(pairs):
        we = w[:, IDX_DIM + 2 * p:IDX_DIM + 2 * p + 1]
        wo = w[:, IDX_DIM + 2 * p + 1:IDX_DIM + 2 * p + 2]
        term = we * jnp.maximum(de[p * t:(p + 1) * t], 0.0) + wo * jnp.maximum(do[p * t:(p + 1) * t], 0.0)
        acc = term if acc is None else acc + term
    return acc


def _masked_attention(q, bias, kb, vb):
    t = q.shape[0]
    bias4 = jnp.concatenate([bias] * GROUP, axis=0)
    outs = []
    for n in range(N_KV_HEADS):
        q4 = jnp.concatenate([q[:, (n * GROUP + g) * HEAD_DIM:(n * GROUP + g + 1) * HEAD_DIM]
                              for g in range(GROUP)], axis=0)
        s = lax.dot_general(q4, kb[:, n * HEAD_DIM:(n + 1) * HEAD_DIM], NT_DIMS,
                            preferred_element_type=F32) * HEAD_DIM ** -0.5 + bias4
        m = jnp.max(s, axis=1, keepdims=True)
        e = jnp.exp(s - m)
        l = jnp.sum(e, axis=1, keepdims=True)
        o = jnp.dot(e.astype(BF16), vb[:, n * HEAD_DIM:(n + 1) * HEAD_DIM], preferred_element_type=F32) / l
        outs.extend(o[g * t:(g + 1) * t] for g in range(GROUP))
    return outs


def _attn_p_kernel(q_ref, qi_ref, kwq_ref, kws_ref, k_ref, v_ref, o_ref,
                   kke_ref, kko_ref, kb_ref, vb_ref, key_ref, bias_ref, *, topk, kc):
    i = pl.program_id(1)
    tq = q_ref.shape[0]
    s_len = k_ref.shape[0]

    @pl.when(i == 0)
    def _():
        kw = kws_ref[...]
        lane = lax.broadcasted_iota(I32, kw.shape, 1)
        ke = jnp.where(lane < IDX_DIM, kw, 0.0)
        kke_ref[...] = ke.astype(BF16)
        kko_ref[...] = pltpu.roll(ke, IDX_DIM, axis=1).astype(BF16)
        kb_ref[...] = k_ref[...].astype(BF16)
        vb_ref[...] = v_ref[...].astype(BF16)

    qi = qi_ref[...].astype(BF16)
    w = kwq_ref[...] * (IDX_HEADS ** -0.5 * IDX_DIM ** -0.5)
    for c in range(s_len // kc):
        sl = slice(c * kc, (c + 1) * kc)
        sc = _index_scores(qi, w, kke_ref[sl, :], kko_ref[sl, :])
        row = lax.broadcasted_iota(I32, (tq, kc), 0) + i * tq
        col = lax.broadcasted_iota(I32, (tq, kc), 1) + c * kc
        key_ref[:, sl] = _sortable(jnp.where(col <= row, sc, NEG_INF))

    def causal(first_col, width):
        row = lax.broadcasted_iota(I32, (tq, width), 0) + i * tq
        col = lax.broadcasted_iota(I32, (tq, width), 1) + first_col
        return col <= row

    _topk_bias(key_ref, bias_ref, causal, topk)

    outs = _masked_attention(q_ref[...].astype(BF16), bias_ref[...], kb_ref[...], vb_ref[...])
    for h, o in enumerate(outs):
        o_ref[:, h * HEAD_DIM:(h + 1) * HEAD_DIM] = o.astype(o_ref.dtype)


def _attn_prompt(proj, lay, batch, seq, tq):
    nq = seq // tq
    topk = min(TOPK_MAX, seq // 4)
    kc = min(512, seq)
    kern = functools.partial(_attn_p_kernel, topk=topk, kc=kc)
    return pl.pallas_call(
        kern,
        out_shape=jax.ShapeDtypeStruct((batch * seq, D_Q), BF16),
        grid=(batch, nq),
        in_specs=[pl.BlockSpec((tq, D_Q), lambda b, i: (b * nq + i, lay.q // D_Q)),
                  pl.BlockSpec((tq, D_QI), lambda b, i: (b * nq + i, lay.qi // D_QI)),
                  pl.BlockSpec((tq, KIWI), lambda b, i: (b * nq + i, lay.kiwi // KIWI)),
                  pl.BlockSpec((seq, KIWI), lambda b, i: (b, lay.kiwi // KIWI)),
                  pl.BlockSpec((seq, D_KV), lambda b, i: (b, lay.k // D_KV)),
                  pl.BlockSpec((seq, D_KV), lambda b, i: (b, lay.v // D_KV))],
        out_specs=pl.BlockSpec((tq, D_Q), lambda b, i: (b * nq + i, 0)),
        scratch_shapes=[pltpu.VMEM((seq, KIWI), BF16), pltpu.VMEM((seq, KIWI), BF16),
                        pltpu.VMEM((seq, D_KV), BF16), pltpu.VMEM((seq, D_KV), BF16),
                        pltpu.VMEM((tq, seq), I32), pltpu.VMEM((tq, seq), F32)],
        compiler_params=_cparams("parallel", "arbitrary"),
        name="attn_p",
    )(proj, proj, proj, proj, proj, proj)


def _attn_s_kernel(pt_ref, row_ref, ckidx_ref, ck_ref, cv_ref, o_ref,
                   kidx_buf, k_buf, v_buf, key_ref, bias_ref, sem_i, sem_k, sem_v,
                   *, lay, n_pages, t_new, topk):
    b = pl.program_id(0)
    nb = pl.num_programs(0)
    past = n_pages * PAGE_SIZE
    l_pad = past + PAGE_SIZE
    slot = b % 2

    def kidx_copy(seq, sl, p):
        return pltpu.make_async_copy(ckidx_ref.at[pt_ref[seq, p]],
                                     kidx_buf.at[sl, pl.ds(p * PAGE_SIZE, PAGE_SIZE), :], sem_i.at[sl])

    def k_copy(p):
        return pltpu.make_async_copy(ck_ref.at[pt_ref[b, p]], k_buf.at[pl.ds(p * PAGE_SIZE, PAGE_SIZE), :], sem_k)

    def v_copy(p):
        return pltpu.make_async_copy(cv_ref.at[pt_ref[b, p]], v_buf.at[pl.ds(p * PAGE_SIZE, PAGE_SIZE), :], sem_v)

    @pl.when(b == 0)
    def _():
        for p in range(n_pages):
            kidx_copy(0, 0, p).start()
        kidx_buf[:, past:, :] = jnp.zeros((2, PAGE_SIZE, IDX_DIM), F32)
        k_buf[past:, :] = jnp.zeros((PAGE_SIZE, D_KV), F32)
        v_buf[past:, :] = jnp.zeros((PAGE_SIZE, D_KV), F32)

    for p in range(n_pages):
        k_copy(p).start()
        v_copy(p).start()

    @pl.when(b + 1 < nb)
    def _():
        for p in range(n_pages):
            kidx_copy(b + 1, 1 - slot, p).start()

    rows = row_ref[...]
    kiwi = rows[:, lay.kiwi:lay.kiwi + KIWI]
    for p in range(n_pages):
        kidx_copy(b, slot, p).wait()
    kidx_buf[slot, past:past + t_new, :] = kiwi[:, :IDX_DIM]

    ki = kidx_buf[slot].astype(BF16)
    zeros = jnp.zeros_like(ki)
    kke = jnp.concatenate([ki, zeros], axis=1)
    kko = jnp.concatenate([zeros, ki], axis=1)
    qi = rows[:, lay.qi:lay.qi + D_QI].astype(BF16)
    w = kiwi * (IDX_HEADS ** -0.5 * IDX_DIM ** -0.5)
    sc = _index_scores(qi, w, kke, kko)

    def causal(first_col, width):
        row = lax.broadcasted_iota(I32, (t_new, width), 0) + past
        col = lax.broadcasted_iota(I32, (t_new, width), 1) + first_col
        return col <= row

    key_ref[...] = _sortable(jnp.where(causal(0, l_pad), sc, NEG_INF))
    _topk_bias(key_ref, bias_ref, causal, topk)

    for p in range(n_pages):
        k_copy(p).wait()
        v_copy(p).wait()
    k_buf[past:past + t_new, :] = rows[:, lay.k:lay.k + D_KV]
    v_buf[past:past + t_new, :] = rows[:, lay.v:lay.v + D_KV]
    q = rows[:, lay.q:lay.q + D_Q].astype(BF16)
    outs = _masked_attention(q, bias_ref[...], k_buf[...].astype(BF16), v_buf[...].astype(BF16))
    for h, o in enumerate(outs):
        o_ref[:, h * HEAD_DIM:(h + 1) * HEAD_DIM] = o


def _attn_sample(proj_s, lay, page_table, cache_kidx, cache_k, cache_v, t_new):
    db, n_pages = page_table.shape
    n_pool = cache_k.shape[0]
    l_pad = (n_pages + 1) * PAGE_SIZE
    topk = min(TOPK_MAX, (n_pages * PAGE_SIZE + t_new) // 4)
    kern = functools.partial(_attn_s_kernel, lay=lay, n_pages=n_pages, t_new=t_new, topk=topk)
    grid_spec = pltpu.PrefetchScalarGridSpec(
        num_scalar_prefetch=1,
        grid=(db,),
        in_specs=[pl.BlockSpec((None, t_new, lay.total), lambda b, pt: (b, 0, 0)),
                  pl.BlockSpec(memory_space=pl.ANY),
                  pl.BlockSpec(memory_space=pl.ANY),
                  pl.BlockSpec(memory_space=pl.ANY)],
        out_specs=pl.BlockSpec((None, t_new, D_Q), lambda b, pt: (b, 0, 0)),
        scratch_shapes=[pltpu.VMEM((2, l_pad, IDX_DIM), F32),
                        pltpu.VMEM((l_pad, D_KV), F32),
                        pltpu.VMEM((l_pad, D_KV), F32),
                        pltpu.VMEM((t_new, l_pad), I32),
                        pltpu.VMEM((t_new, l_pad), F32),
                        pltpu.SemaphoreType.DMA((2,)),
                        pltpu.SemaphoreType.DMA(()),
                        pltpu.SemaphoreType.DMA(())])
    return pl.pallas_call(
        kern,
        out_shape=jax.ShapeDtypeStruct((db, t_new, D_Q), F32),
        grid_spec=grid_spec,
        compiler_params=_cparams("arbitrary"),
        name="attn_s",
    )(page_table, proj_s.reshape(db, t_new, lay.total), cache_kidx,
      cache_k.reshape(n_pool, PAGE_SIZE, D_KV), cache_v.reshape(n_pool, PAGE_SIZE, D_KV))


def _conv_tail(ext_ref, rows, first, wdw_ref, bdw_ref, lng_ref, lnb_ref):
    acc = None
    for j in range(CONV_WIDTH):
        term = wdw_ref[j:j + 1, :] * ext_ref[first + j:first + j + rows, :]
        acc = term if acc is None else acc + term
    y = acc + bdw_ref[...]
    mu = jnp.mean(y, axis=-1, keepdims=True)
    yc = y - mu
    var = jnp.mean(yc * yc, axis=-1, keepdims=True)
    return _silu(yc * lax.rsqrt(var + EPS) * lng_ref[...] + lnb_ref[...])


def _conv_p_kernel(a_ref, b_ref, ah_ref, bh_ref, wdw_ref, bdw_ref, lng_ref, lnb_ref,
                   cs_ref, tail_ref, ext_ref):
    i = pl.program_id(1)
    tm = a_ref.shape[0]
    hist = CONV_WIDTH - 1
    uh = ah_ref[...] * _sigmoid(bh_ref[...])
    ext_ref[0:HALO, :] = jnp.where(i == 0, 0.0, uh)
    ext_ref[HALO:, :] = a_ref[...] * _sigmoid(b_ref[...])
    cs_ref[...] = _conv_tail(ext_ref, tm, HALO - hist, wdw_ref, bdw_ref, lng_ref, lnb_ref).astype(cs_ref.dtype)

    @pl.when(i == pl.num_programs(1) - 1)
    def _():
        tail_ref[...] = ext_ref[HALO + tm - hist:HALO + tm, :]


def _conv_prompt(proj, lay, batch, seq, dc, tm, w_dw, b_dw, ln_g, ln_b):
    nt = seq // tm
    hb = tm // HALO
    hist = CONV_WIDTH - 1

    def halo_map(col):
        return lambda b, i: (jnp.maximum((b * nt + i) * hb - 1, 0), col)

    vec = pl.BlockSpec((1, dc), lambda b, i: (0, 0))
    return pl.pallas_call(
        _conv_p_kernel,
        out_shape=(jax.ShapeDtypeStruct((batch * seq, dc), BF16),
                   jax.ShapeDtypeStruct((batch, hist, dc), F32)),
        grid=(batch, nt),
        in_specs=[pl.BlockSpec((tm, dc), lambda b, i: (b * nt + i, lay.a // dc)),
                  pl.BlockSpec((tm, dc), lambda b, i: (b * nt + i, lay.b // dc)),
                  pl.BlockSpec((HALO, dc), halo_map(lay.a // dc)),
                  pl.BlockSpec((HALO, dc), halo_map(lay.b // dc)),
                  pl.BlockSpec((CONV_WIDTH, dc), lambda b, i: (0, 0)),
                  vec, vec, vec],
        out_specs=(pl.BlockSpec((tm, dc), lambda b, i: (b * nt + i, 0)),
                   pl.BlockSpec((None, hist, dc), lambda b, i: (b, 0, 0))),
        scratch_shapes=[pltpu.VMEM((HALO + tm, dc), F32)],
        compiler_params=_cparams("parallel", "arbitrary"),
        name="conv_p",
    )(proj, proj, proj, proj, w_dw, b_dw, ln_g, ln_b)


def _conv_s_kernel(row_ref, st_ref, wdw_ref, bdw_ref, lng_ref, lnb_ref, cs_ref, u_ref, ext_ref, *, lay, dc):
    t_new = row_ref.shape[0]
    hist = CONV_WIDTH - 1
    rows = row_ref[...]
    u = rows[:, lay.a:lay.a + dc] * _sigmoid(rows[:, lay.b:lay.b + dc])
    ext_ref[0:hist, :] = st_ref[...]
    ext_ref[hist:hist + t_new, :] = u
    u_ref[...] = u
    cs_ref[...] = _conv_tail(ext_ref, t_new, 0, wdw_ref, bdw_ref, lng_ref, lnb_ref)


def _conv_sample(proj_s, lay, state, dc, t_new, w_dw, b_dw, ln_g, ln_b):
    db = state.shape[0]
    hist = CONV_WIDTH - 1
    vec = pl.BlockSpec((1, dc), lambda b: (0, 0))
    kern = functools.partial(_conv_s_kernel, lay=lay, dc=dc)
    return pl.pallas_call(
        kern,
        out_shape=(jax.ShapeDtypeStruct((db, t_new, dc), F32),
                   jax.ShapeDtypeStruct((db, t_new, dc), F32)),
        grid=(db,),
        in_specs=[pl.BlockSpec((None, t_new, lay.total), lambda b: (b, 0, 0)),
                  pl.BlockSpec((None, hist, dc), lambda b: (b, 0, 0)),
                  pl.BlockSpec((CONV_WIDTH, dc), lambda b: (0, 0)),
                  vec, vec, vec],
        out_specs=(pl.BlockSpec((None, t_new, dc), lambda b: (b, 0, 0)),
                   pl.BlockSpec((None, t_new, dc), lambda b: (b, 0, 0))),
        scratch_shapes=[pltpu.VMEM((_round_up(hist + t_new, 8), dc), F32)],
        compiler_params=_cparams("parallel"),
        name="conv_s",
    )(proj_s.reshape(db, t_new, lay.total), state, w_dw, b_dw, ln_g, ln_b)


def _merge_kernel(ao_ref, cs_ref, ga_ref, gb_ref, x_ref, gate_ref, g_ref, woa_ref, wpw_ref, wo_ref, o_ref):
    a = jnp.dot(ao_ref[...].astype(BF16), woa_ref[...], preferred_element_type=F32)
    c = jnp.dot(cs_ref[...].astype(BF16), wpw_ref[...], preferred_element_type=F32)
    merged = _sigmoid(ga_ref[...]) * a + _sigmoid(gb_ref[...]) * c
    y = jnp.dot(merged.astype(BF16), wo_ref[...], preferred_element_type=F32)
    o_ref[...] = x_ref[...] + gate_ref[...] * _rms(y, g_ref[...])


def _resident(shape):
    return pl.BlockSpec(shape, lambda i: (0,) * len(shape), pipeline_mode=pl.Buffered(1))


def _merge(ao, cs, proj, lay, x, mod, g_post, w_oa, w_pw2, w_o, tm):
    n, d = x.shape
    dc = cs.shape[1]
    in_specs = [pl.BlockSpec((tm, D_Q), lambda i: (i, 0)),
                pl.BlockSpec((tm, dc), lambda i: (i, 0)),
                pl.BlockSpec((tm, d), lambda i: (i, lay.ga // d)),
                pl.BlockSpec((tm, d), lambda i: (i, lay.gb // d)),
                pl.BlockSpec((tm, d), lambda i: (i, 0)),
                mod.spec(2, tm, d, 1),
                _resident((1, d)), _resident((D_Q, d)), _resident((dc, d)), _resident((d, d))]
    return pl.pallas_call(
        _merge_kernel,
        out_shape=jax.ShapeDtypeStruct((n, d), F32),
        grid=(n // tm,),
        in_specs=in_specs,
        out_specs=pl.BlockSpec((tm, d), lambda i: (i, 0)),
        compiler_params=_cparams("parallel"),
        name="merge",
    )(ao, cs, proj, proj, x, mod.arr, g_post, w_oa, w_pw2, w_o)


def _first_index(mask, iota, size):
    return jnp.min(jnp.where(mask, iota, size), axis=0, keepdims=True)


def _route_kernel(hall_ref, x_ref, shift_ref, scale_ref, g_ref, wrt_ref, br_ref, wsg_ref, wsu_ref, wsd_ref,
                  h_ref, eidx_ref, ew_ref, ysh_ref):
    del hall_ref
    tm = x_ref.shape[0]
    h = _rms(x_ref[...], g_ref[...]) * (1.0 + scale_ref[...]) + shift_ref[...]
    h_ref[...] = h
    hb = h.astype(BF16)

    gate = jnp.dot(hb, wsg_ref[...], preferred_element_type=F32)
    up = jnp.dot(hb, wsu_ref[...], preferred_element_type=F32)
    ysh_ref[...] = jnp.dot((_silu(gate) * up).astype(BF16), wsd_ref[...], preferred_element_type=F32)

    logits = lax.dot_general(wrt_ref[...], hb, NT_DIMS, preferred_element_type=F32)
    s = _sigmoid(logits)
    sel = s + br_ref[...]
    gsz = N_EXPERTS // N_GROUPS
    iota_g = lax.broadcasted_iota(I32, (gsz, tm), 0)
    gscores = []
    for g in range(N_GROUPS):
        slab = sel[g * gsz:(g + 1) * gsz]
        m1 = jnp.max(slab, axis=0, keepdims=True)
        first = _first_index(slab == m1, iota_g, gsz)
        m2 = jnp.max(jnp.where(iota_g == first, NEG_INF, slab), axis=0, keepdims=True)
        gscores.append(m1 + m2)
    gs = jnp.concatenate(gscores, axis=0)
    iota_n = lax.broadcasted_iota(I32, (N_GROUPS, tm), 0)
    gmask = jnp.zeros((N_GROUPS, tm), F32)
    for _ in range(TOPK_GROUPS):
        m = jnp.max(gs, axis=0, keepdims=True)
        pick = iota_n == _first_index(gs == m, iota_n, N_GROUPS)
        gmask = jnp.where(pick, 1.0, gmask)
        gs = jnp.where(pick, NEG_INF, gs)
    gmask_e = jnp.concatenate([jnp.broadcast_to(gmask[g:g + 1], (gsz, tm)) for g in range(N_GROUPS)], axis=0)
    selm = jnp.where(gmask_e > 0.5, sel, NEG_INF)
    iota_e = lax.broadcasted_iota(I32, (N_EXPERTS, tm), 0)
    idxs, ws = [], []
    for _ in range(TOP_K):
        m = jnp.max(selm, axis=0, keepdims=True)
        first = _first_index(selm == m, iota_e, N_EXPERTS)
        pick = iota_e == first
        idxs.append(first)
        ws.append(jnp.sum(jnp.where(pick, s, 0.0), axis=0, keepdims=True))
        selm = jnp.where(pick, NEG_INF, selm)
    wsel = jnp.concatenate(ws, axis=0)
    eidx_ref[...] = jnp.concatenate(idxs, axis=0)
    ew_ref[...] = wsel / jnp.sum(wsel, axis=0, keepdims=True) * ROUTED_SCALE


def _route(h_all, x1, mod, g_pre, w_rt, b_r, w_sg, w_su, w_sd, tm, row_off):
    n, d = x1.shape
    hs = w_sg.shape[1]
    off = row_off // tm
    in_specs = [pl.BlockSpec(memory_space=pl.ANY),
                pl.BlockSpec((tm, d), lambda i: (i, 0)),
                mod.spec(3, tm, d, 1),
                mod.spec(4, tm, d, 1),
                _resident((1, d)), _resident((N_EXPERTS, d)), _resident((N_EXPERTS, 1)),
                _resident((d, hs)), _resident((d, hs)), _resident((hs, d))]
    return pl.pallas_call(
        _route_kernel,
        out_shape=(jax.ShapeDtypeStruct(h_all.shape, F32),
                   jax.ShapeDtypeStruct((TOP_K, n), I32),
                   jax.ShapeDtypeStruct((TOP_K, n), F32),
                   jax.ShapeDtypeStruct((n, d), F32)),
        grid=(n // tm,),
        in_specs=in_specs,
        out_specs=(pl.BlockSpec((tm, d), lambda i: (i + off, 0)),
                   pl.BlockSpec((TOP_K, tm), lambda i: (0, i)),
                   pl.BlockSpec((TOP_K, tm), lambda i: (0, i)),
                   pl.BlockSpec((tm, d), lambda i: (i, 0))),
        input_output_aliases={0: 0},
        compiler_params=_cparams("parallel"),
        name="route",
    )(h_all, x1, mod.arr, mod.arr, g_pre, w_rt, b_r, w_sg, w_su, w_sd)


def _expert_kernel(blk_e_ref, nused_ref, rows_ref, h_ref, wg_ref, wu_ref, wd_ref, yb_ref,
                   idx_smem, xbuf, wgb, wub, wdb, sem_idx, sem_x):
    i = pl.program_id(0)
    n_used = nused_ref[0]
    blk = xbuf.shape[1]
    slot = i % 2

    def idx_copy(j, sl):
        return pltpu.make_async_copy(rows_ref.at[j], idx_smem.at[sl], sem_idx.at[sl])

    def issue_rows(sl):
        for a in range(blk // IDX_LANES):
            def body(r, carry, a=a):
                tok = idx_smem[sl, a, r]
                pltpu.make_async_copy(h_ref.at[pl.ds(tok, 1), :],
                                      xbuf.at[sl, pl.ds(a * IDX_LANES + r, 1), :], sem_x.at[sl]).start()
                return carry
            lax.fori_loop(0, IDX_LANES, body, 0)

    def wait_rows(sl):
        pltpu.make_async_copy(h_ref.at[pl.ds(0, blk), :], xbuf.at[sl], sem_x.at[sl]).wait()

    @pl.when(i == 0)
    def _():
        idx_copy(0, 0).start()
        idx_copy(0, 0).wait()
        issue_rows(0)

        @pl.when(1 < n_used)
        def _():
            idx_copy(1, 1).start()

    @pl.when(i + 1 < n_used)
    def _():
        idx_copy(i + 1, 1 - slot).wait()
        issue_rows(1 - slot)

    @pl.when(i + 2 < n_used)
    def _():
        idx_copy(i + 2, slot).start()

    changed = jnp.logical_or(i == 0, blk_e_ref[i] != blk_e_ref[jnp.maximum(i - 1, 0)])

    @pl.when(jnp.logical_and(changed, i < n_used))
    def _():
        wgb[...] = wg_ref[...].astype(BF16)
        wub[...] = wu_ref[...].astype(BF16)
        wdb[...] = wd_ref[...].astype(BF16)

    @pl.when(i < n_used)
    def _():
        wait_rows(slot)
        x = xbuf[slot].astype(BF16)
        gate = jnp.dot(x, wgb[...], preferred_element_type=F32)
        up = jnp.dot(x, wub[...], preferred_element_type=F32)
        yb_ref[...] = jnp.dot((_silu(gate) * up).astype(BF16), wdb[...], preferred_element_type=F32)

    @pl.when(i >= n_used)
    def _():
        yb_ref[...] = jnp.zeros_like(yb_ref)


def _experts(h_all, blk_e, n_used, rows2d, w_gate, w_up, w_down):
    n_all, d = h_all.shape
    n_blocks, blk = rows2d.shape
    hid = w_gate.shape[2]
    rows3d = rows2d.reshape(n_blocks, blk // IDX_LANES, IDX_LANES)
    grid_spec = pltpu.PrefetchScalarGridSpec(
        num_scalar_prefetch=2,
        grid=(n_blocks,),
        in_specs=[pl.BlockSpec(rows3d.shape, lambda i, be, nu: (0, 0, 0)),
                  pl.BlockSpec(memory_space=pl.ANY),
                  pl.BlockSpec((None, d, hid), lambda i, be, nu: (be[i], 0, 0)),
                  pl.BlockSpec((None, d, hid), lambda i, be, nu: (be[i], 0, 0)),
                  pl.BlockSpec((None, hid, d), lambda i, be, nu: (be[i], 0, 0))],
        out_specs=pl.BlockSpec((blk, d), lambda i, be, nu: (i, 0)),
        scratch_shapes=[pltpu.SMEM((2, blk // IDX_LANES, IDX_LANES), I32),
                        pltpu.VMEM((2, blk, d), F32),
                        pltpu.VMEM((d, hid), BF16), pltpu.VMEM((d, hid), BF16), pltpu.VMEM((hid, d), BF16),
                        pltpu.SemaphoreType.DMA((2,)), pltpu.SemaphoreType.DMA((2,))])
    return pl.pallas_call(
        _expert_kernel,
        out_shape=jax.ShapeDtypeStruct((n_blocks * blk, d), F32),
        grid_spec=grid_spec,
        compiler_params=_cparams("arbitrary"),
        name="expert",
    )(blk_e, n_used, rows3d, h_all, w_gate, w_up, w_down)


def _combine_kernel(pos_ref, yb_ref, ew_ref, ysh_ref, x_ref, gate_ref, g_ref, o_ref,
                    idx_smem, cbuf, sem_idx, sem_c, *, tile_off):
    i = pl.program_id(0)
    nt = pl.num_programs(0)
    tm = x_ref.shape[0]
    slot = i % 2

    def idx_copy(j, sl):
        return pltpu.make_async_copy(pos_ref.at[j + tile_off], idx_smem.at[sl], sem_idx.at[sl])

    def issue_rows(sl):
        for k in range(TOP_K):
            def body(r, carry, k=k):
                p = idx_smem[sl, k, r]
                pltpu.make_async_copy(yb_ref.at[pl.ds(p, 1), :],
                                      cbuf.at[sl, pl.ds(k * tm + r, 1), :], sem_c.at[sl]).start()
                return carry
            lax.fori_loop(0, tm, body, 0)

    def wait_rows(sl):
        pltpu.make_async_copy(yb_ref.at[pl.ds(0, TOP_K * tm), :], cbuf.at[sl], sem_c.at[sl]).wait()

    @pl.when(i == 0)
    def _():
        idx_copy(0, 0).start()
        idx_copy(0, 0).wait()
        issue_rows(0)

        @pl.when(1 < nt)
        def _():
            idx_copy(1, 1).start()

    @pl.when(i + 1 < nt)
    def _():
        idx_copy(i + 1, 1 - slot).wait()
        issue_rows(1 - slot)

    @pl.when(i + 2 < nt)
    def _():
        idx_copy(i + 2, slot).start()

    wait_rows(slot)
    ew = ew_ref[...]
    y = ysh_ref[...]
    for k in range(TOP_K):
        y = y + ew[:, k:k + 1] * cbuf[slot, k * tm:(k + 1) * tm, :]
    o_ref[...] = x_ref[...] + gate_ref[...] * _rms(y, g_ref[...])


def _combine(pos_tiles, yb, ew_nt, ysh, x1, mod, g_post, tm, row_off):
    n, d = x1.shape
    off = row_off // tm
    n_tiles = pos_tiles.shape[0]
    kern = functools.partial(_combine_kernel, tile_off=off)
    return pl.pallas_call(
        kern,
        out_shape=jax.ShapeDtypeStruct((n, d), F32),
        grid=(n // tm,),
        in_specs=[pl.BlockSpec((n_tiles, TOP_K, tm), lambda i: (0, 0, 0)),
                  pl.BlockSpec(memory_space=pl.ANY),
                  pl.BlockSpec((tm, TOP_K), lambda i: (i, 0)),
                  pl.BlockSpec((tm, d), lambda i: (i, 0)),
                  pl.BlockSpec((tm, d), lambda i: (i, 0)),
                  mod.spec(5, tm, d, 1),
                  pl.BlockSpec((1, d), lambda i: (0, 0))],
        out_specs=pl.BlockSpec((tm, d), lambda i: (i, 0)),
        scratch_shapes=[pltpu.SMEM((2, TOP_K, tm), I32),
                        pltpu.VMEM((2, TOP_K * tm, d), F32),
                        pltpu.SemaphoreType.DMA((2,)), pltpu.SemaphoreType.DMA((2,))],
        compiler_params=_cparams("arbitrary"),
        name="combine",
    )(pos_tiles, yb, ew_nt, ysh, x1, mod.arr, g_post)


def _routing_tables(eidx_t, n_all, blk):
    m = TOP_K * n_all
    flat_e = eidx_t.reshape(m)
    order = jnp.argsort(flat_e, stable=True).astype(I32)
    e_sorted = flat_e[order]
    tok = order % n_all
    counts = jnp.zeros((N_EXPERTS,), I32).at[flat_e].add(1)
    padded = (counts + blk - 1) // blk * blk
    pad_end = jnp.cumsum(padded)
    start = jnp.cumsum(counts) - counts
    dest = (pad_end - padded)[e_sorted] + jnp.arange(m, dtype=I32) - start[e_sorted]
    n_blocks = -(-m // blk) + N_EXPERTS
    rows = jnp.zeros((n_blocks * blk,), I32).at[dest].set(tok)
    blk_e = jnp.minimum(jnp.searchsorted(pad_end, jnp.arange(n_blocks, dtype=I32) * blk, side="right"),
                        N_EXPERTS - 1).astype(I32)
    pos = jnp.zeros((m,), I32).at[order].set(dest).reshape(TOP_K, n_all)
    n_used = (pad_end[-1] // blk).astype(I32).reshape(1)
    return rows.reshape(n_blocks, blk), blk_e, n_used, pos


def _pick(n, prefs):
    for t in prefs:
        if n % t == 0:
            return t
    return n


def kernel(x_prompt, x_sample, cache_k, cache_v, cache_kidx, state_conv, page_table, c_prompt, c_sample,
           w_ada, b_ada, g_mix_pre, g_mix_post, w_in, w_oa, w_dw, b_dw, ln_conv_g, ln_conv_b, w_pw2, w_o,
           g_ffn_pre, g_ffn_post, w_router, b_router, w_exp_gate, w_exp_up, w_exp_down,
           w_sh_gate, w_sh_up, w_sh_down):
    batch, seq, d = x_prompt.shape
    db, t_new, _ = x_sample.shape
    depth = w_ada.shape[0]
    dc = w_dw.shape[2]
    hist = CONV_WIDTH - 1
    n_p, n_s = batch * seq, db * t_new
    n_all = n_p + n_s
    lay = _Layout(d, dc)

    xp = x_prompt.reshape(n_p, d)
    xs = x_sample.reshape(n_s, d)
    c_all = jnp.concatenate([c_prompt, c_sample], axis=0)
    outs = {k: [] for k in ("kp", "vp", "ip", "cp", "ks", "vs", "is", "cs")}

    for l in range(depth):
        m_all = _ada(c_all, w_ada[l], b_ada[l])
        mod_p = _Mod(m_all[:batch].reshape(batch, 6, 1, d), per_token=False, rows_per_seq=seq)
        mod_s = _Mod(jnp.repeat(m_all[batch:], t_new, axis=0), per_token=True)

        w_in_b = _permute_w_in(w_in[l], d, dc, lay)
        w_oa_b, w_pw2_b, w_o_b = w_oa[l].astype(BF16), w_pw2[l].astype(BF16), w_o[l].astype(BF16)
        w_rt_b = w_router[l].T.astype(BF16)
        b_r = b_router[l].reshape(N_EXPERTS, 1)
        w_sg_b, w_su_b, w_sd_b = w_sh_gate[l].astype(BF16), w_sh_up[l].astype(BF16), w_sh_down[l].astype(BF16)
        g1, g2 = g_mix_pre[l].reshape(1, d), g_mix_post[l].reshape(1, d)
        g3, g4 = g_ffn_pre[l].reshape(1, d), g_ffn_post[l].reshape(1, d)
        bdw, lng, lnb = b_dw[l].reshape(1, dc), ln_conv_g[l].reshape(1, dc), ln_conv_b[l].reshape(1, dc)

        tm_proj = _pick(seq, (1024, 512, 256, 128))
        proj_p = _proj(xp, mod_p, g1, w_in_b, tm_proj, lay.tn)
        proj_s = _proj(xs, mod_s, g1, w_in_b, n_s, lay.tn)

        tq = _pick(seq, (256, 128))
        ao_p = _attn_prompt(proj_p, lay, batch, seq, tq)
        ao_s = _attn_sample(proj_s, lay, page_table, cache_kidx[l], cache_k[l], cache_v[l], t_new)

        tm_c = _pick(seq, (256, 128))
        cs_p, tail_p = _conv_prompt(proj_p, lay, batch, seq, dc, tm_c, w_dw[l], bdw, lng, lnb)
        cs_s, u_s = _conv_sample(proj_s, lay, state_conv[l], dc, t_new, w_dw[l], bdw, lng, lnb)

        tm_m = _pick(seq, (256, 128))
        x1_p = _merge(ao_p, cs_p, proj_p, lay, xp, mod_p, g2, w_oa_b, w_pw2_b, w_o_b, tm_m)
        x1_s = _merge(ao_s.reshape(n_s, D_Q), cs_s.reshape(n_s, dc), proj_s, lay, xs, mod_s, g2,
                      w_oa_b, w_pw2_b, w_o_b, n_s)

        tm_r = _pick(seq, (256, 128))
        h_all = jnp.zeros((n_all, d), F32)
        h_all, eidx_p, ew_p, ysh_p = _route(h_all, x1_p, mod_p, g3, w_rt_b, b_r, w_sg_b, w_su_b, w_sd_b, tm_r, 0)
        h_all, eidx_s, ew_s, ysh_s = _route(h_all, x1_s, mod_s, g3, w_rt_b, b_r, w_sg_b, w_su_b, w_sd_b, n_s, n_p)

        eidx_t = jnp.concatenate([eidx_p, eidx_s], axis=1)
        rows2d, blk_e, n_used, pos = _routing_tables(eidx_t, n_all, EXPERT_BLOCK)
        yb = _experts(h_all, blk_e, n_used, rows2d, w_exp_gate[l], w_exp_up[l], w_exp_down[l])

        tm_k = n_s
        pos_tiles = pos.reshape(TOP_K, n_all // tm_k, tm_k).transpose(1, 0, 2)
        xp = _combine(pos_tiles, yb, ew_p.T, ysh_p, x1_p, mod_p, g4, tm_k, 0)
        xs = _combine(pos_tiles, yb, ew_s.T, ysh_s, x1_s, mod_s, g4, tm_k, n_p)

        outs["kp"].append(proj_p[:, lay.k:lay.k + D_KV].reshape(batch, seq, N_KV_HEADS, HEAD_DIM))
        outs["vp"].append(proj_p[:, lay.v:lay.v + D_KV].reshape(batch, seq, N_KV_HEADS, HEAD_DIM))
        outs["ip"].append(proj_p[:, lay.kiwi:lay.kiwi + IDX_DIM].reshape(batch, seq, IDX_DIM))
        outs["cp"].append(tail_p)
        outs["ks"].append(proj_s[:, lay.k:lay.k + D_KV].reshape(db, t_new, N_KV_HEADS, HEAD_DIM))
        outs["vs"].append(proj_s[:, lay.v:lay.v + D_KV].reshape(db, t_new, N_KV_HEADS, HEAD_DIM))
        outs["is"].append(proj_s[:, lay.kiwi:lay.kiwi + IDX_DIM].reshape(db, t_new, IDX_DIM))
        outs["cs"].append(jnp.concatenate([state_conv[l], u_s], axis=1)[:, -hist:])

    st = {k: jnp.stack(v) for k, v in outs.items()}
    return (xp.reshape(batch, seq, d), xs.reshape(db, t_new, d),
            st["kp"], st["vp"], st["ip"], st["cp"], st["ks"], st["vs"], st["is"], st["cs"])
```

```python
import functools

import jax
import jax.numpy as jnp
from jax import lax
from jax.experimental import pallas as pl
from jax.experimental.pallas import tpu as pltpu

F32 = jnp.float32
BF16 = jnp.bfloat16
I32 = jnp.int32

N_HEADS = 8
N_KV_HEADS = 2
HEAD_DIM = 128
IDX_HEADS = 16
IDX_DIM = 64
TOPK_MAX = 256
CONV_WIDTH = 31
N_EXPERTS = 64
N_GROUPS = 8
TOPK_GROUPS = 4
TOP_K = 8
ROUTED_SCALE = 2.5
PAGE_SIZE = 128
EPS = 1e-6

D_Q = N_HEADS * HEAD_DIM
D_KV = N_KV_HEADS * HEAD_DIM
D_QI = IDX_HEADS * IDX_DIM
GROUP = N_HEADS // N_KV_HEADS
KIWI = 128
HALO = 32
EXPERT_BLOCK = 256
IDX_LANES = 128
INT_MIN = -(2 ** 31)
NEG_INF = float("-inf")
NEG_INF_KEY = -2139095041
VMEM_LIMIT = 56 * 1024 * 1024

NT_DIMS = (((1,), (1,)), ((), ()))


def _cparams(*sem):
    return pltpu.CompilerParams(dimension_semantics=sem, vmem_limit_bytes=VMEM_LIMIT)


def _round_up(x, m):
    return (x + m - 1) // m * m


def _rms(x, g):
    return x * lax.rsqrt(jnp.mean(x * x, axis=-1, keepdims=True) + EPS) * g


def _sigmoid(x):
    return 1.0 / (1.0 + jnp.exp(-x))


def _silu(x):
    return x * _sigmoid(x)


def _key_to_float(key):
    bits = key ^ ((key >> 31) & 0x7FFFFFFF)
    return jnp.where(key < NEG_INF_KEY, NEG_INF, pltpu.bitcast(bits, F32))


def _kth_largest(sc_ref, k):
    rows = sc_ref.shape[0]

    def body(it, prefix):
        cand = prefix | lax.shift_left(jnp.int32(1), 31 - it)
        cnt = jnp.sum((sc_ref[...] >= _key_to_float(cand ^ INT_MIN)).astype(I32), axis=1, keepdims=True)
        return jnp.where(cnt >= k, cand, prefix)

    prefix = lax.fori_loop(0, 32, body, jnp.zeros((rows, 1), I32))
    return _key_to_float(prefix ^ INT_MIN)


def _topk_bias(sc_ref, bias_ref, valid_fn, k):
    rows, cols = sc_ref.shape
    thr = _kth_largest(sc_ref, k)
    sc = sc_ref[...]
    ge = sc >= thr
    bias_ref[...] = jnp.where(valid_fn(0, cols), jnp.where(ge, 0.0, NEG_INF), NEG_INF)
    n_ge = jnp.sum(ge.astype(I32), axis=1, keepdims=True)
    tied = jnp.where(n_ge > k, jnp.where(thr > NEG_INF, 1, 0), 0)

    @pl.when(jnp.max(tied) > 0)
    def _():
        need = (k - jnp.sum((sc > thr).astype(I32), axis=1, keepdims=True)).astype(F32)
        r = lax.broadcasted_iota(I32, (128, 128), 0)
        c = lax.broadcasted_iota(I32, (128, 128), 1)
        tri = jnp.where(r < c, 1.0, 0.0).astype(BF16)
        carry = jnp.zeros((rows, 1), F32)
        for j in range(cols // 128):
            sl = slice(j * 128, (j + 1) * 128)
            sc_c = sc_ref[:, sl]
            eq_c = jnp.where(sc_c == thr, 1.0, 0.0)
            before = carry + jnp.dot(eq_c.astype(BF16), tri, preferred_element_type=F32)
            keep_tied = jnp.where(sc_c > thr, 1.0, jnp.where(before < need, eq_c, 0.0))
            keep = jnp.where(tied > 0, keep_tied, jnp.where(sc_c >= thr, 1.0, 0.0))
            bias_ref[:, sl] = jnp.where(valid_fn(j * 128, 128), jnp.where(keep > 0.5, 0.0, NEG_INF), NEG_INF)
            carry = carry + jnp.sum(eq_c, axis=1, keepdims=True)


def _ada_kernel(c_ref, w_ref, b_ref, o_ref):
    o_ref[...] = jnp.dot(c_ref[...].astype(BF16), w_ref[...].astype(BF16),
                         preferred_element_type=F32) + b_ref[...]


def _ada(c, w_ada, b_ada, layer):
    n, d = c.shape
    cols = w_ada.shape[2]
    tn = _pick(cols, (1024, 512, 256, 128))
    return pl.pallas_call(
        _ada_kernel,
        out_shape=jax.ShapeDtypeStruct((n, cols), F32),
        grid=(cols // tn,),
        in_specs=[pl.BlockSpec((n, d), lambda j: (0, 0)),
                  pl.BlockSpec((None, d, tn), lambda j: (layer, 0, j)),
                  pl.BlockSpec((1, tn), lambda j: (0, j))],
        out_specs=pl.BlockSpec((n, tn), lambda j: (0, j)),
        compiler_params=_cparams("parallel"),
        name="ada",
    )(c, w_ada, b_ada.reshape(1, cols))


class _Mod:
    def __init__(self, arr, per_token, rows_per_seq=None):
        self.arr = arr
        self.per_token = per_token
        self.rows_per_seq = rows_per_seq

    def spec(self, which, tm, d, ngrid):
        if self.per_token:
            if ngrid == 1:
                return pl.BlockSpec((tm, d), lambda i: (i, which))
            return pl.BlockSpec((tm, d), lambda i, j: (i, which))
        per = self.rows_per_seq // tm
        if ngrid == 1:
            return pl.BlockSpec((None, None, 1, d), lambda i: (i // per, which, 0, 0))
        return pl.BlockSpec((None, None, 1, d), lambda i, j: (i // per, which, 0, 0))


def _proj_kernel(x_ref, shift_ref, scale_ref, g_ref, w_ref, o_ref, h_ref):
    @pl.when(pl.program_id(1) == 0)
    def _():
        y = _rms(x_ref[...], g_ref[...])
        h_ref[...] = (y * (1.0 + scale_ref[...]) + shift_ref[...]).astype(BF16)

    o_ref[...] = jnp.dot(h_ref[...], w_ref[...], preferred_element_type=F32)


def _proj(x, mod, g, w, tm, tn):
    n, d = x.shape
    cols = w.shape[1]
    return pl.pallas_call(
        _proj_kernel,
        out_shape=jax.ShapeDtypeStruct((n, cols), F32),
        grid=(n // tm, cols // tn),
        in_specs=[pl.BlockSpec((tm, d), lambda i, j: (i, 0)),
                  mod.spec(0, tm, d, 2),
                  mod.spec(1, tm, d, 2),
                  pl.BlockSpec((1, d), lambda i, j: (0, 0)),
                  pl.BlockSpec((d, tn), lambda i, j: (0, j))],
        out_specs=pl.BlockSpec((tm, tn), lambda i, j: (i, j)),
        scratch_shapes=[pltpu.VMEM((tm, d), BF16)],
        compiler_params=_cparams("parallel", "arbitrary"),
        name="proj",
    )(x, mod.arr, mod.arr, g, w)


class _Layout:
    def __init__(self, d, dc):
        self.q = 0
        self.qi = D_Q
        self.ga = D_Q + D_QI
        self.gb = self.ga + d
        self.a = self.gb + d
        self.b = self.a + dc
        self.k = self.b + dc
        self.v = self.k + D_KV
        self.kiwi = self.v + D_KV
        used = self.kiwi + KIWI
        self.tn = 1024 if used >= 4096 else 512
        self.total = _round_up(used, self.tn)
        assert self.ga % d == 0 and self.a % dc == 0 and self.k % D_KV == 0


def _permute_w_in(w_in, d, dc, lay):
    offs = [0]
    for s in (D_Q, D_KV, D_KV, D_QI, IDX_DIM, IDX_HEADS, 2 * dc, d, d):
        offs.append(offs[-1] + s)
    q, k, v, qi, ki, wi, conv, ga, gb = [w_in[:, offs[i]:offs[i + 1]] for i in range(9)]
    pad1 = jnp.zeros((d, KIWI - IDX_DIM - IDX_HEADS), w_in.dtype)
    pad2 = jnp.zeros((d, lay.total - lay.kiwi - KIWI), w_in.dtype)
    return jnp.concatenate([q, qi, ga, gb, conv, k, v, ki, wi, pad1, pad2], axis=1).astype(BF16)


def _index_scores(qi, w, kke, kko):
    t = qi.shape[0]
    pairs = IDX_HEADS // 2
    qs = jnp.concatenate([qi[:, p * 128:(p + 1) * 128] for p in range(pairs)], axis=0)
    de = lax.dot_general(qs, kke, NT_DIMS, preferred_element_type=F32)
    do = lax.dot_general(qs, kko, NT_DIMS, preferred_element_type=F32)
    acc = None
    for p in range(pairs):
        we = w[:, IDX_DIM + 2 * p:IDX_DIM + 2 * p + 1]
        wo = w[:, IDX_DIM + 2 * p + 1:IDX_DIM + 2 * p + 2]
        term = we * jnp.maximum(de[p * t:(p + 1) * t], 0.0) + wo * jnp.maximum(do[p * t:(p + 1) * t], 0.0)
        acc = term if acc is None else acc + term
    return acc


def _masked_attention(q, bias, kb, vb):
    t = q.shape[0]
    bias4 = jnp.concatenate([bias] * GROUP, axis=0)
    outs = []
    for n in range(N_KV_HEADS):
        q4 = jnp.concatenate([q[:, (n * GROUP + g) * HEAD_DIM:(n * GROUP + g + 1) * HEAD_DIM]
                              for g in range(GROUP)], axis=0)
        s = lax.dot_general(q4, kb[:, n * HEAD_DIM:(n + 1) * HEAD_DIM], NT_DIMS,
                            preferred_element_type=F32) * HEAD_DIM ** -0.5 + bias4
        m = jnp.max(s, axis=1, keepdims=True)
        e = jnp.exp(s - m)
        l = jnp.sum(e, axis=1, keepdims=True)
        o = jnp.dot(e.astype(BF16), vb[:, n * HEAD_DIM:(n + 1) * HEAD_DIM], preferred_element_type=F32) / l
        outs.extend(o[g * t:(g + 1) * t] for g in range(GROUP))
    return outs


def _attn_p_kernel(q_ref, qi_ref, kwq_ref, kws_ref, k_ref, v_ref, o_ref,
                   kke_ref, kko_ref, kb_ref, vb_ref, sc_ref, bias_ref, *, topk, kc):
    i = pl.program_id(1)
    tq = q_ref.shape[0]
    s_len = k_ref.shape[0]

    @pl.when(i == 0)
    def _():
        kw = kws_ref[...]
        lane = lax.broadcasted_iota(I32, kw.shape, 1)
        ke = jnp.where(lane < IDX_DIM, kw, 0.0)
        kke_ref[...] = ke.astype(BF16)
        kko_ref[...] = pltpu.roll(ke, IDX_DIM, axis=1).astype(BF16)
        kb_ref[...] = k_ref[...].astype(BF16)
        vb_ref[...] = v_ref[...].astype(BF16)

    qi = qi_ref[...].astype(BF16)
    w = kwq_ref[...] * (IDX_HEADS ** -0.5 * IDX_DIM ** -0.5)
    for c in range(s_len // kc):
        sl = slice(c * kc, (c + 1) * kc)
        sc = _index_scores(qi, w, kke_ref[sl, :], kko_ref[sl, :])
        row = lax.broadcasted_iota(I32, (tq, kc), 0) + i * tq
        col = lax.broadcasted_iota(I32, (tq, kc), 1) + c * kc
        sc_ref[:, sl] = jnp.where(col <= row, sc, NEG_INF)

    def causal(first_col, width):
        row = lax.broadcasted_iota(I32, (tq, width), 0) + i * tq
        col = lax.broadcasted_iota(I32, (tq, width), 1) + first_col
        return col <= row

    _topk_bias(sc_ref, bias_ref, causal, topk)

    outs = _masked_attention(q_ref[...].astype(BF16), bias_ref[...], kb_ref[...], vb_ref[...])
    for h, o in enumerate(outs):
        o_ref[:, h * HEAD_DIM:(h + 1) * HEAD_DIM] = o.astype(o_ref.dtype)


def _attn_prompt(proj, lay, batch, seq, tq):
    nq = seq // tq
    topk = min(TOPK_MAX, seq // 4)
    kc = min(512, seq)
    kern = functools.partial(_attn_p_kernel, topk=topk, kc=kc)
    return pl.pallas_call(
        kern,
        out_shape=jax.ShapeDtypeStruct((batch * seq, D_Q), BF16),
        grid=(batch, nq),
        in_specs=[pl.BlockSpec((tq, D_Q), lambda b, i: (b * nq + i, lay.q // D_Q)),
                  pl.BlockSpec((tq, D_QI), lambda b, i: (b * nq + i, lay.qi // D_QI)),
                  pl.BlockSpec((tq, KIWI), lambda b, i: (b * nq + i, lay.kiwi // KIWI)),
                  pl.BlockSpec((seq, KIWI), lambda b, i: (b, lay.kiwi // KIWI)),
                  pl.BlockSpec((seq, D_KV), lambda b, i: (b, lay.k // D_KV)),
                  pl.BlockSpec((seq, D_KV), lambda b, i: (b, lay.v // D_KV))],
        out_specs=pl.BlockSpec((tq, D_Q), lambda b, i: (b * nq + i, 0)),
        scratch_shapes=[pltpu.VMEM((seq, KIWI), BF16), pltpu.VMEM((seq, KIWI), BF16),
                        pltpu.VMEM((seq, D_KV), BF16), pltpu.VMEM((seq, D_KV), BF16),
                        pltpu.VMEM((tq, seq), F32), pltpu.VMEM((tq, seq), F32)],
        compiler_params=_cparams("parallel", "arbitrary"),
        name="attn_p",
    )(proj, proj, proj, proj, proj, proj)


def _attn_s_kernel(pt_ref, row_ref, ckidx_ref, ck_ref, cv_ref, o_ref,
                   kidx_buf, k_buf, v_buf, sc_ref, bias_ref, sem_i, sem_k, sem_v,
                   *, lay, n_pages, t_new, topk):
    b = pl.program_id(0)
    nb = pl.num_programs(0)
    past = n_pages * PAGE_SIZE
    l_pad = past + PAGE_SIZE
    slot = b % 2

    def kidx_copy(seq, sl, p):
        return pltpu.make_async_copy(ckidx_ref.at[pt_ref[seq, p]],
                                     kidx_buf.at[sl, pl.ds(p * PAGE_SIZE, PAGE_SIZE), :], sem_i.at[sl])

    def k_copy(p):
        return pltpu.make_async_copy(ck_ref.at[pt_ref[b, p]], k_buf.at[pl.ds(p * PAGE_SIZE, PAGE_SIZE), :], sem_k)

    def v_copy(p):
        return pltpu.make_async_copy(cv_ref.at[pt_ref[b, p]], v_buf.at[pl.ds(p * PAGE_SIZE, PAGE_SIZE), :], sem_v)

    @pl.when(b == 0)
    def _():
        for p in range(n_pages):
            kidx_copy(0, 0, p).start()
        kidx_buf[:, past:, :] = jnp.zeros((2, PAGE_SIZE, IDX_DIM), F32)
        k_buf[past:, :] = jnp.zeros((PAGE_SIZE, D_KV), F32)
        v_buf[past:, :] = jnp.zeros((PAGE_SIZE, D_KV), F32)

    for p in range(n_pages):
        k_copy(p).start()
        v_copy(p).start()

    @pl.when(b + 1 < nb)
    def _():
        for p in range(n_pages):
            kidx_copy(b + 1, 1 - slot, p).start()

    rows = row_ref[...]
    kiwi = rows[:, lay.kiwi:lay.kiwi + KIWI]
    for p in range(n_pages):
        kidx_copy(b, slot, p).wait()
    kidx_buf[slot, past:past + t_new, :] = kiwi[:, :IDX_DIM]

    ki = kidx_buf[slot].astype(BF16)
    zeros = jnp.zeros_like(ki)
    kke = jnp.concatenate([ki, zeros], axis=1)
    kko = jnp.concatenate([zeros, ki], axis=1)
    qi = rows[:, lay.qi:lay.qi + D_QI].astype(BF16)
    w = kiwi * (IDX_HEADS ** -0.5 * IDX_DIM ** -0.5)
    sc = _index_scores(qi, w, kke, kko)

    def causal(first_col, width):
        row = lax.broadcasted_iota(I32, (t_new, width), 0) + past
        col = lax.broadcasted_iota(I32, (t_new, width), 1) + first_col
        return col <= row

    sc_ref[...] = jnp.where(causal(0, l_pad), sc, NEG_INF)
    _topk_bias(sc_ref, bias_ref, causal, topk)

    for p in range(n_pages):
        k_copy(p).wait()
        v_copy(p).wait()
    k_buf[past:past + t_new, :] = rows[:, lay.k:lay.k + D_KV]
    v_buf[past:past + t_new, :] = rows[:, lay.v:lay.v + D_KV]
    q = rows[:, lay.q:lay.q + D_Q].astype(BF16)
    outs = _masked_attention(q, bias_ref[...], k_buf[...].astype(BF16), v_buf[...].astype(BF16))
    for h, o in enumerate(outs):
        o_ref[:, h * HEAD_DIM:(h + 1) * HEAD_DIM] = o


def _attn_sample(proj_s, lay, page_table, cache_kidx, cache_k, cache_v, t_new, layer):
    db, n_pages = page_table.shape
    depth, n_pool = cache_k.shape[:2]
    page_table = page_table + layer * n_pool
    n_pool = depth * n_pool
    l_pad = (n_pages + 1) * PAGE_SIZE
    topk = min(TOPK_MAX, (n_pages * PAGE_SIZE + t_new) // 4)
    kern = functools.partial(_attn_s_kernel, lay=lay, n_pages=n_pages, t_new=t_new, topk=topk)
    grid_spec = pltpu.PrefetchScalarGridSpec(
        num_scalar_prefetch=1,
        grid=(db,),
        in_specs=[pl.BlockSpec((None, t_new, lay.total), lambda b, pt: (b, 0, 0)),
                  pl.BlockSpec(memory_space=pl.ANY),
                  pl.BlockSpec(memory_space=pl.ANY),
                  pl.BlockSpec(memory_space=pl.ANY)],
        out_specs=pl.BlockSpec((None, t_new, D_Q), lambda b, pt: (b, 0, 0)),
        scratch_shapes=[pltpu.VMEM((2, l_pad, IDX_DIM), F32),
                        pltpu.VMEM((l_pad, D_KV), F32),
                        pltpu.VMEM((l_pad, D_KV), F32),
                        pltpu.VMEM((t_new, l_pad), F32),
                        pltpu.VMEM((t_new, l_pad), F32),
                        pltpu.SemaphoreType.DMA((2,)),
                        pltpu.SemaphoreType.DMA(()),
                        pltpu.SemaphoreType.DMA(())])
    return pl.pallas_call(
        kern,
        out_shape=jax.ShapeDtypeStruct((db, t_new, D_Q), F32),
        grid_spec=grid_spec,
        compiler_params=_cparams("arbitrary"),
        name="attn_s",
    )(page_table, proj_s.reshape(db, t_new, lay.total), cache_kidx.reshape(n_pool, PAGE_SIZE, IDX_DIM),
      cache_k.reshape(n_pool, PAGE_SIZE, D_KV), cache_v.reshape(n_pool, PAGE_SIZE, D_KV))


def _conv_tail(ext_ref, rows, first, wdw_ref, bdw_ref, lng_ref, lnb_ref):
    acc = None
    for j in range(CONV_WIDTH):
        term = wdw_ref[j:j + 1, :] * ext_ref[first + j:first + j + rows, :]
        acc = term if acc is None else acc + term
    y = acc + bdw_ref[...]
    mu = jnp.mean(y, axis=-1, keepdims=True)
    yc = y - mu
    var = jnp.mean(yc * yc, axis=-1, keepdims=True)
    return _silu(yc * lax.rsqrt(var + EPS) * lng_ref[...] + lnb_ref[...])


def _conv_p_kernel(a_ref, b_ref, ah_ref, bh_ref, wdw_ref, bdw_ref, lng_ref, lnb_ref,
                   cs_ref, tail_ref, ext_ref):
    i = pl.program_id(1)
    tm = a_ref.shape[0]
    hist = CONV_WIDTH - 1
    uh = ah_ref[...] * _sigmoid(bh_ref[...])
    ext_ref[0:HALO, :] = jnp.where(i == 0, 0.0, uh)
    ext_ref[HALO:, :] = a_ref[...] * _sigmoid(b_ref[...])
    cs_ref[...] = _conv_tail(ext_ref, tm, HALO - hist, wdw_ref, bdw_ref, lng_ref, lnb_ref).astype(cs_ref.dtype)

    @pl.when(i == pl.num_programs(1) - 1)
    def _():
        tail_ref[...] = ext_ref[HALO + tm - hist:HALO + tm, :]


def _conv_prompt(proj, lay, batch, seq, dc, tm, w_dw, b_dw, ln_g, ln_b):
    nt = seq // tm
    hb = tm // HALO
    hist = CONV_WIDTH - 1

    def halo_map(col):
        return lambda b, i: (jnp.maximum((b * nt + i) * hb - 1, 0), col)

    vec = pl.BlockSpec((1, dc), lambda b, i: (0, 0))
    return pl.pallas_call(
        _conv_p_kernel,
        out_shape=(jax.ShapeDtypeStruct((batch * seq, dc), BF16),
                   jax.ShapeDtypeStruct((batch, hist, dc), F32)),
        grid=(batch, nt),
        in_specs=[pl.BlockSpec((tm, dc), lambda b, i: (b * nt + i, lay.a // dc)),
                  pl.BlockSpec((tm, dc), lambda b, i: (b * nt + i, lay.b // dc)),
                  pl.BlockSpec((HALO, dc), halo_map(lay.a // dc)),
                  pl.BlockSpec((HALO, dc), halo_map(lay.b // dc)),
                  pl.BlockSpec((CONV_WIDTH, dc), lambda b, i: (0, 0)),
                  vec, vec, vec],
        out_specs=(pl.BlockSpec((tm, dc), lambda b, i: (b * nt + i, 0)),
                   pl.BlockSpec((None, hist, dc), lambda b, i: (b, 0, 0))),
        scratch_shapes=[pltpu.VMEM((HALO + tm, dc), F32)],
        compiler_params=_cparams("parallel", "arbitrary"),
        name="conv_p",
    )(proj, proj, proj, proj, w_dw, b_dw, ln_g, ln_b)


def _conv_s_kernel(row_ref, st_ref, wdw_ref, bdw_ref, lng_ref, lnb_ref, cs_ref, u_ref, ext_ref, *, lay, dc):
    t_new = row_ref.shape[0]
    hist = CONV_WIDTH - 1
    rows = row_ref[...]
    u = rows[:, lay.a:lay.a + dc] * _sigmoid(rows[:, lay.b:lay.b + dc])
    ext_ref[0:hist, :] = st_ref[...]
    ext_ref[hist:hist + t_new, :] = u
    u_ref[...] = u
    cs_ref[...] = _conv_tail(ext_ref, t_new, 0, wdw_ref, bdw_ref, lng_ref, lnb_ref)


def _conv_sample(proj_s, lay, state, dc, t_new, w_dw, b_dw, ln_g, ln_b):
    db = state.shape[0]
    hist = CONV_WIDTH - 1
    vec = pl.BlockSpec((1, dc), lambda b: (0, 0))
    kern = functools.partial(_conv_s_kernel, lay=lay, dc=dc)
    return pl.pallas_call(
        kern,
        out_shape=(jax.ShapeDtypeStruct((db, t_new, dc), F32),
                   jax.ShapeDtypeStruct((db, t_new, dc), F32)),
        grid=(db,),
        in_specs=[pl.BlockSpec((None, t_new, lay.total), lambda b: (b, 0, 0)),
                  pl.BlockSpec((None, hist, dc), lambda b: (b, 0, 0)),
                  pl.BlockSpec((CONV_WIDTH, dc), lambda b: (0, 0)),
                  vec, vec, vec],
        out_specs=(pl.BlockSpec((None, t_new, dc), lambda b: (b, 0, 0)),
                   pl.BlockSpec((None, t_new, dc), lambda b: (b, 0, 0))),
        scratch_shapes=[pltpu.VMEM((_round_up(hist + t_new, 8), dc), F32)],
        compiler_params=_cparams("parallel"),
        name="conv_s",
    )(proj_s.reshape(db, t_new, lay.total), state, w_dw, b_dw, ln_g, ln_b)


def _merge_kernel(ao_ref, cs_ref, ga_ref, gb_ref, x_ref, gate_ref, g_ref, woa_ref, wpw_ref, wo_ref, o_ref):
    a = jnp.dot(ao_ref[...].astype(BF16), woa_ref[...], preferred_element_type=F32)
    c = jnp.dot(cs_ref[...].astype(BF16), wpw_ref[...], preferred_element_type=F32)
    merged = _sigmoid(ga_ref[...]) * a + _sigmoid(gb_ref[...]) * c
    y = jnp.dot(merged.astype(BF16), wo_ref[...], preferred_element_type=F32)
    o_ref[...] = x_ref[...] + gate_ref[...] * _rms(y, g_ref[...])


def _resident(shape):
    return pl.BlockSpec(shape, lambda i: (0,) * len(shape), pipeline_mode=pl.Buffered(1))


def _merge(ao, cs, proj, lay, x, mod, g_post, w_oa, w_pw2, w_o, tm):
    n, d = x.shape
    dc = cs.shape[1]
    in_specs = [pl.BlockSpec((tm, D_Q), lambda i: (i, 0)),
                pl.BlockSpec((tm, dc), lambda i: (i, 0)),
                pl.BlockSpec((tm, d), lambda i: (i, lay.ga // d)),
                pl.BlockSpec((tm, d), lambda i: (i, lay.gb // d)),
                pl.BlockSpec((tm, d), lambda i: (i, 0)),
                mod.spec(2, tm, d, 1),
                _resident((1, d)), _resident((D_Q, d)), _resident((dc, d)), _resident((d, d))]
    return pl.pallas_call(
        _merge_kernel,
        out_shape=jax.ShapeDtypeStruct((n, d), F32),
        grid=(n // tm,),
        in_specs=in_specs,
        out_specs=pl.BlockSpec((tm, d), lambda i: (i, 0)),
        compiler_params=_cparams("parallel"),
        name="merge",
    )(ao, cs, proj, proj, x, mod.arr, g_post, w_oa, w_pw2, w_o)


def _first_index(mask, iota, size):
    return jnp.min(jnp.where(mask, iota, size), axis=0, keepdims=True)


def _route_kernel(hall_ref, x_ref, shift_ref, scale_ref, g_ref, wrt_ref, br_ref, wsg_ref, wsu_ref, wsd_ref,
                  h_ref, eidx_ref, ew_ref, ysh_ref, hist_ref):
    del hall_ref
    tm = x_ref.shape[0]

    @pl.when(pl.program_id(0) == 0)
    def _():
        hist_ref[...] = jnp.zeros_like(hist_ref)

    h = _rms(x_ref[...], g_ref[...]) * (1.0 + scale_ref[...]) + shift_ref[...]
    h_ref[...] = h
    hb = h.astype(BF16)

    gate = jnp.dot(hb, wsg_ref[...], preferred_element_type=F32)
    up = jnp.dot(hb, wsu_ref[...], preferred_element_type=F32)
    ysh_ref[...] = jnp.dot((_silu(gate) * up).astype(BF16), wsd_ref[...], preferred_element_type=F32)

    logits = lax.dot_general(wrt_ref[...], hb, NT_DIMS, preferred_element_type=F32)
    s = _sigmoid(logits)
    sel = s + br_ref[...]
    gsz = N_EXPERTS // N_GROUPS
    iota_g = lax.broadcasted_iota(I32, (gsz, tm), 0)
    gscores = []
    for g in range(N_GROUPS):
        slab = sel[g * gsz:(g + 1) * gsz]
        m1 = jnp.max(slab, axis=0, keepdims=True)
        first = _first_index(slab == m1, iota_g, gsz)
        m2 = jnp.max(jnp.where(iota_g == first, NEG_INF, slab), axis=0, keepdims=True)
        gscores.append(m1 + m2)
    gs = jnp.concatenate(gscores, axis=0)
    iota_n = lax.broadcasted_iota(I32, (N_GROUPS, tm), 0)
    gmask = jnp.zeros((N_GROUPS, tm), F32)
    for _ in range(TOPK_GROUPS):
        m = jnp.max(gs, axis=0, keepdims=True)
        pick = iota_n == _first_index(gs == m, iota_n, N_GROUPS)
        gmask = jnp.where(pick, 1.0, gmask)
        gs = jnp.where(pick, NEG_INF, gs)
    gmask_e = jnp.concatenate([jnp.broadcast_to(gmask[g:g + 1], (gsz, tm)) for g in range(N_GROUPS)], axis=0)
    selm = jnp.where(gmask_e > 0.5, sel, NEG_INF)
    iota_e = lax.broadcasted_iota(I32, (N_EXPERTS, tm), 0)
    idxs, ws = [], []
    chosen = jnp.zeros((N_EXPERTS, tm), I32)
    for _ in range(TOP_K):
        m = jnp.max(selm, axis=0, keepdims=True)
        first = _first_index(selm == m, iota_e, N_EXPERTS)
        pick = iota_e == first
        idxs.append(first)
        ws.append(jnp.sum(jnp.where(pick, s, 0.0), axis=0, keepdims=True))
        selm = jnp.where(pick, NEG_INF, selm)
        chosen = jnp.where(pick, 1, chosen)
    wsel = jnp.concatenate(ws, axis=0)
    eidx_ref[...] = jnp.concatenate(idxs, axis=0)
    ew_ref[...] = wsel / jnp.sum(wsel, axis=0, keepdims=True) * ROUTED_SCALE
    hist_ref[...] += jnp.sum(chosen, axis=1, keepdims=True)


def _route(h_all, x1, mod, g_pre, w_rt, b_r, w_sg, w_su, w_sd, tm, row_off):
    n, d = x1.shape
    hs = w_sg.shape[1]
    off = row_off // tm
    in_specs = [pl.BlockSpec(memory_space=pl.ANY),
                pl.BlockSpec((tm, d), lambda i: (i, 0)),
                mod.spec(3, tm, d, 1),
                mod.spec(4, tm, d, 1),
                _resident((1, d)), _resident((N_EXPERTS, d)), _resident((N_EXPERTS, 1)),
                _resident((d, hs)), _resident((d, hs)), _resident((hs, d))]
    return pl.pallas_call(
        _route_kernel,
        out_shape=(jax.ShapeDtypeStruct(h_all.shape, F32),
                   jax.ShapeDtypeStruct((TOP_K, n), I32),
                   jax.ShapeDtypeStruct((TOP_K, n), F32),
                   jax.ShapeDtypeStruct((n, d), F32),
                   jax.ShapeDtypeStruct((N_EXPERTS, 1), I32)),
        grid=(n // tm,),
        in_specs=in_specs,
        out_specs=(pl.BlockSpec((tm, d), lambda i: (i + off, 0)),
                   pl.BlockSpec((TOP_K, tm), lambda i: (0, i)),
                   pl.BlockSpec((TOP_K, tm), lambda i: (0, i)),
                   pl.BlockSpec((tm, d), lambda i: (i, 0)),
                   pl.BlockSpec((N_EXPERTS, 1), lambda i: (0, 0))),
        input_output_aliases={0: 0},
        compiler_params=_cparams("arbitrary"),
        name="route",
    )(h_all, x1, mod.arr, mod.arr, g_pre, w_rt, b_r, w_sg, w_su, w_sd)


def _expert_kernel(se_ref, st_ref, lo_ref, hi_ref, ns_ref, tbl_ref, h_ref, wg_ref, wu_ref, wd_ref, y_ref,
                   idx_smem, xbuf, acc, wgb, wub, wdb, sem_idx, sem_x, sem_y, *, n_tiles):
    s = pl.program_id(0)
    blk = xbuf.shape[1]
    nsl = blk // IDX_LANES
    t = st_ref[s]
    par = t % 2
    lo, hi = lo_ref[s], hi_ref[s]
    live = s < ns_ref[0]
    last = hi == blk

    def idx_copy(tile, sl):
        return pltpu.make_async_copy(tbl_ref.at[tile], idx_smem.at[sl], sem_idx.at[sl])

    def gather_rows(sl):
        for a in range(nsl):
            for r in range(IDX_LANES):
                tok = idx_smem[sl, a, r]
                pltpu.make_async_copy(h_ref.at[pl.ds(tok, 1), :],
                                      xbuf.at[sl, pl.ds(a * IDX_LANES + r, 1), :], sem_x.at[sl]).start()

    def scatter_rows(sl):
        for a in range(nsl):
            for r in range(IDX_LANES):
                dst = idx_smem[sl, nsl + a, r]
                pltpu.make_async_copy(acc.at[sl, pl.ds(a * IDX_LANES + r, 1), :],
                                      y_ref.at[pl.ds(dst, 1), :], sem_y.at[sl]).start()

    def wait_gather(sl):
        pltpu.make_async_copy(h_ref.at[pl.ds(0, blk), :], xbuf.at[sl], sem_x.at[sl]).wait()

    def wait_scatter(sl):
        pltpu.make_async_copy(acc.at[sl], y_ref.at[pl.ds(0, blk), :], sem_y.at[sl]).wait()

    @pl.when(jnp.logical_and(s == 0, live))
    def _():
        acc[...] = jnp.zeros_like(acc)
        idx_copy(0, 0).start()
        idx_copy(0, 0).wait()
        gather_rows(0)
        idx_copy(min(1, n_tiles - 1), 1).start()
        pltpu.make_async_copy(acc.at[0], acc.at[1], sem_y.at[1]).start()
        wait_gather(0)

    changed = jnp.logical_or(s == 0, se_ref[s] != se_ref[jnp.maximum(s - 1, 0)])

    @pl.when(jnp.logical_and(live, changed))
    def _():
        wgb[...] = wg_ref[...].astype(BF16)
        wub[...] = wu_ref[...].astype(BF16)
        wdb[...] = wd_ref[...].astype(BF16)

    def compute():
        x = xbuf[par].astype(BF16)
        gate = jnp.dot(x, wgb[...], preferred_element_type=F32)
        up = jnp.dot(x, wub[...], preferred_element_type=F32)
        y = jnp.dot((_silu(gate) * up).astype(BF16), wdb[...], preferred_element_type=F32)
        row = lax.broadcasted_iota(I32, (blk, 1), 0)
        mine = jnp.where(row >= lo, jnp.where(row < hi, 1, 0), 0)
        acc[par] = jnp.where(mine > 0, y, acc[par])

    @pl.when(jnp.logical_and(live, last))
    def _():
        nxt = jnp.minimum(t + 1, n_tiles - 1)
        idx_copy(nxt, 1 - par).wait()
        gather_rows(1 - par)
        wait_scatter(1 - par)
        compute()
        scatter_rows(par)
        idx_copy(jnp.minimum(t + 2, n_tiles - 1), par).start()
        wait_gather(1 - par)

        @pl.when(t == n_tiles - 1)
        def _():
            wait_scatter(par)
            idx_copy(n_tiles - 1, par).wait()

    @pl.when(jnp.logical_and(live, jnp.logical_not(last)))
    def _():
        compute()


def _experts(h_all, tbl, meta, w_gate, w_up, w_down, layer):
    n_all, d = h_all.shape
    n_tiles = tbl.shape[0]
    blk = tbl.shape[1] // 2 * IDX_LANES
    hid = w_gate.shape[3]
    se, st, lo, hi, ns = meta

    def wspec(shape):
        return pl.BlockSpec((None, None) + shape, lambda s, se, st, lo, hi, ns: (layer, se[s], 0, 0))

    grid_spec = pltpu.PrefetchScalarGridSpec(
        num_scalar_prefetch=5,
        grid=(se.shape[0],),
        in_specs=[pl.BlockSpec(tbl.shape, lambda s, *_: (0, 0, 0)),
                  pl.BlockSpec(memory_space=pl.ANY),
                  wspec((d, hid)), wspec((d, hid)), wspec((hid, d))],
        out_specs=pl.BlockSpec(memory_space=pl.ANY),
        scratch_shapes=[pltpu.SMEM((2,) + tbl.shape[1:], I32),
                        pltpu.VMEM((2, blk, d), F32), pltpu.VMEM((2, blk, d), F32),
                        pltpu.VMEM((d, hid), BF16), pltpu.VMEM((d, hid), BF16), pltpu.VMEM((hid, d), BF16),
                        pltpu.SemaphoreType.DMA((2,)), pltpu.SemaphoreType.DMA((2,)),
                        pltpu.SemaphoreType.DMA((2,))])
    return pl.pallas_call(
        functools.partial(_expert_kernel, n_tiles=n_tiles),
        out_shape=jax.ShapeDtypeStruct((n_tiles * blk, d), F32),
        grid_spec=grid_spec,
        compiler_params=_cparams("arbitrary"),
        name="expert",
    )(se, st, lo, hi, ns, tbl, h_all, w_gate, w_up, w_down)


def _combine_kernel(y_ref, ew_ref, ysh_ref, x_ref, gate_ref, g_ref, o_ref):
    ew = ew_ref[...]
    y = ysh_ref[...]
    for k in range(TOP_K):
        y = y + ew[:, k:k + 1] * y_ref[k]
    o_ref[...] = x_ref[...] + gate_ref[...] * _rms(y, g_ref[...])


def _combine(y_slots, ew_nt, ysh, x1, mod, g_post, tm, row_off):
    n, d = x1.shape
    off = row_off // tm
    return pl.pallas_call(
        _combine_kernel,
        out_shape=jax.ShapeDtypeStruct((n, d), F32),
        grid=(n // tm,),
        in_specs=[pl.BlockSpec((TOP_K, tm, d), lambda i: (0, i + off, 0)),
                  pl.BlockSpec((tm, TOP_K), lambda i: (i, 0)),
                  pl.BlockSpec((tm, d), lambda i: (i, 0)),
                  pl.BlockSpec((tm, d), lambda i: (i, 0)),
                  mod.spec(5, tm, d, 1),
                  pl.BlockSpec((1, d), lambda i: (0, 0))],
        out_specs=pl.BlockSpec((tm, d), lambda i: (i, 0)),
        compiler_params=_cparams("parallel"),
        name="combine",
    )(y_slots, ew_nt, ysh, x1, mod.arr, g_post)


def _routing_tables(eidx_t, counts, n_all, blk):
    m = TOP_K * n_all
    assert m % blk == 0
    n_tiles = m // blk
    order = jnp.argsort(eidx_t.reshape(m), stable=True).astype(I32)
    tok = order % n_all
    nsl = blk // IDX_LANES
    tbl = jnp.concatenate([tok.reshape(n_tiles, nsl, IDX_LANES), order.reshape(n_tiles, nsl, IDX_LANES)], axis=1)

    ends = jnp.cumsum(counts)
    starts = ends - counts
    first_tile = starts // blk
    tiles_of = jnp.where(counts > 0, (ends - 1) // blk - first_tile + 1, 0)
    step_end = jnp.cumsum(tiles_of)
    step_start = step_end - tiles_of
    s = jnp.arange(n_tiles + N_EXPERTS - 1, dtype=I32)
    e = jnp.minimum(jnp.sum((step_end[None, :] <= s[:, None]).astype(I32), axis=1), N_EXPERTS - 1)
    t = jnp.clip(first_tile[e] + s - step_start[e], 0, n_tiles - 1)
    lo = jnp.clip(starts[e] - t * blk, 0, blk)
    hi = jnp.clip(ends[e] - t * blk, 0, blk)
    as_i32 = lambda a: a.astype(I32)
    return tbl, (as_i32(e), as_i32(t), as_i32(lo), as_i32(hi), as_i32(step_end[-1:]))


def _pick(n, prefs):
    for t in prefs:
        if n % t == 0:
            return t
    return n


def kernel(x_prompt, x_sample, cache_k, cache_v, cache_kidx, state_conv, page_table, c_prompt, c_sample,
           w_ada, b_ada, g_mix_pre, g_mix_post, w_in, w_oa, w_dw, b_dw, ln_conv_g, ln_conv_b, w_pw2, w_o,
           g_ffn_pre, g_ffn_post, w_router, b_router, w_exp_gate, w_exp_up, w_exp_down,
           w_sh_gate, w_sh_up, w_sh_down):
    batch, seq, d = x_prompt.shape
    db, t_new, _ = x_sample.shape
    depth = w_ada.shape[0]
    dc = w_dw.shape[2]
    hist = CONV_WIDTH - 1
    n_p, n_s = batch * seq, db * t_new
    n_all = n_p + n_s
    lay = _Layout(d, dc)

    xp = x_prompt.reshape(n_p, d)
    xs = x_sample.reshape(n_s, d)
    c_all = jnp.concatenate([c_prompt, c_sample], axis=0)
    outs = {k: [] for k in ("kp", "vp", "ip", "cp", "ks", "vs", "is", "cs")}

    for l in range(depth):
        m_all = _ada(c_all, w_ada, b_ada[l], l)
        mod_p = _Mod(m_all[:batch].reshape(batch, 6, 1, d), per_token=False, rows_per_seq=seq)
        mod_s = _Mod(jnp.repeat(m_all[batch:], t_new, axis=0), per_token=True)

        w_in_b = _permute_w_in(w_in[l], d, dc, lay)
        w_oa_b, w_pw2_b, w_o_b = w_oa[l].astype(BF16), w_pw2[l].astype(BF16), w_o[l].astype(BF16)
        w_rt_b = w_router[l].T.astype(BF16)
        b_r = b_router[l].reshape(N_EXPERTS, 1)
        w_sg_b, w_su_b, w_sd_b = w_sh_gate[l].astype(BF16), w_sh_up[l].astype(BF16), w_sh_down[l].astype(BF16)
        g1, g2 = g_mix_pre[l].reshape(1, d), g_mix_post[l].reshape(1, d)
        g3, g4 = g_ffn_pre[l].reshape(1, d), g_ffn_post[l].reshape(1, d)
        bdw, lng, lnb = b_dw[l].reshape(1, dc), ln_conv_g[l].reshape(1, dc), ln_conv_b[l].reshape(1, dc)

        tm_proj = _pick(seq, (1024, 512, 256, 128))
        proj_p = _proj(xp, mod_p, g1, w_in_b, tm_proj, lay.tn)
        proj_s = _proj(xs, mod_s, g1, w_in_b, n_s, lay.tn)

        tq = _pick(seq, (256, 128))
        ao_p = _attn_prompt(proj_p, lay, batch, seq, tq)
        ao_s = _attn_sample(proj_s, lay, page_table, cache_kidx, cache_k, cache_v, t_new, l)

        tm_c = _pick(seq, (256, 128))
        cs_p, tail_p = _conv_prompt(proj_p, lay, batch, seq, dc, tm_c, w_dw[l], bdw, lng, lnb)
        cs_s, u_s = _conv_sample(proj_s, lay, state_conv[l], dc, t_new, w_dw[l], bdw, lng, lnb)

        tm_m = _pick(seq, (256, 128))
        x1_p = _merge(ao_p, cs_p, proj_p, lay, xp, mod_p, g2, w_oa_b, w_pw2_b, w_o_b, tm_m)
        x1_s = _merge(ao_s.reshape(n_s, D_Q), cs_s.reshape(n_s, dc), proj_s, lay, xs, mod_s, g2,
                      w_oa_b, w_pw2_b, w_o_b, n_s)

        tm_r = _pick(seq, (256, 128))
        h_all = jnp.zeros((n_all, d), F32)
        h_all, eidx_p, ew_p, ysh_p, cnt_p = _route(h_all, x1_p, mod_p, g3, w_rt_b, b_r, w_sg_b, w_su_b, w_sd_b,
                                                   tm_r, 0)
        h_all, eidx_s, ew_s, ysh_s, cnt_s = _route(h_all, x1_s, mod_s, g3, w_rt_b, b_r, w_sg_b, w_su_b, w_sd_b,
                                                   n_s, n_p)

        eidx_t = jnp.concatenate([eidx_p, eidx_s], axis=1)
        tbl, meta = _routing_tables(eidx_t, (cnt_p + cnt_s).reshape(N_EXPERTS), n_all, EXPERT_BLOCK)
        y_slots = _experts(h_all, tbl, meta, w_exp_gate, w_exp_up, w_exp_down, l).reshape(TOP_K, n_all, d)

        tm_k = n_s
        xp = _combine(y_slots, ew_p.T, ysh_p, x1_p, mod_p, g4, tm_k, 0)
        xs = _combine(y_slots, ew_s.T, ysh_s, x1_s, mod_s, g4, tm_k, n_p)

        outs["kp"].append(proj_p[:, lay.k:lay.k + D_KV].reshape(batch, seq, N_KV_HEADS, HEAD_DIM))
        outs["vp"].append(proj_p[:, lay.v:lay.v + D_KV].reshape(batch, seq, N_KV_HEADS, HEAD_DIM))
        outs["ip"].append(proj_p[:, lay.kiwi:lay.kiwi + IDX_DIM].reshape(batch, seq, IDX_DIM))
        outs["cp"].append(tail_p)
        outs["ks"].append(proj_s[:, lay.k:lay.k + D_KV].reshape(db, t_new, N_KV_HEADS, HEAD_DIM))
        outs["vs"].append(proj_s[:, lay.v:lay.v + D_KV].reshape(db, t_new, N_KV_HEADS, HEAD_DIM))
        outs["is"].append(proj_s[:, lay.kiwi:lay.kiwi + IDX_DIM].reshape(db, t_new, IDX_DIM))
        outs["cs"].append(jnp.concatenate([state_conv[l], u_s], axis=1)[:, -hist:])

    st = {k: jnp.stack(v) for k, v in outs.items()}
    return (xp.reshape(batch, seq, d), xs.reshape(db, t_new, d),
            st["kp"], st["vp"], st["ip"], st["cp"], st["ks"], st["vs"], st["is"], st["cs"])
```
